```python
import math
import jax, jax.numpy as jnp
from jax import lax
import numpy as np


D_MODEL = 1024
BATCH = 8
SEQ = 4096
DEPTH = 1

M_HEADS = 4
M_DV = 128
M_DQK = 64
M_CHUNK = 64
M_CONV = 4
A_HEADS = 8
A_DH = 64
IDX_HEADS = 8
IDX_DIM = 64
TOPK_MAX = 256
Q_BLOCK = 128
REL_BUCKETS = 32
REL_MAX_DIST = 128
D_FF = 2816
FFN_CONV = 3
EPS = 1e-6

M_WIDTH = M_HEADS * M_DV
A_WIDTH = A_HEADS * A_DH
D_MIX = M_WIDTH + A_WIDTH
IN_SIZES = (M_HEADS * M_DQK, M_HEADS * M_DQK, M_WIDTH, M_WIDTH, M_HEADS, M_HEADS,
            A_WIDTH, A_WIDTH, A_WIDTH, IDX_HEADS * IDX_DIM, IDX_DIM, IDX_HEADS)
D_IN = 2 * M_HEADS * M_DQK + 2 * M_WIDTH + 2 * M_HEADS + 3 * A_WIDTH + IDX_HEADS * IDX_DIM + IDX_DIM + IDX_HEADS

kernel_name = 'hymba_mlstm_dsa_convffn'


def rms_norm(x, g):
    xf = x.astype(jnp.float32)
    y = xf * lax.rsqrt(jnp.mean(xf * xf, axis=-1, keepdims=True) + EPS)
    return (y * g.astype(jnp.float32)).astype(x.dtype)


def causal_dwconv(x, w, b):
    width = w.shape[0]
    s = x.shape[1]
    xp = jnp.pad(x, ((0, 0), (width - 1, 0), (0, 0)))
    y = b + xp[:, 0:s] * w[0]
    for j in range(1, width):
        y = y + xp[:, j:j + s] * w[j]
    return y


def split_cols(z, sizes):
    out, off = [], 0
    for sz in sizes:
        out.append(z[..., off:off + sz])
        off += sz
    return out


def rel_bucket(dist):
    max_exact = REL_BUCKETS // 2
    d = jnp.maximum(dist, 0)
    large = max_exact + (jnp.log(jnp.maximum(d, 1).astype(jnp.float32) / max_exact)
                         / math.log(REL_MAX_DIST / max_exact) * (REL_BUCKETS - max_exact)).astype(jnp.int32)
    large = jnp.minimum(large, REL_BUCKETS - 1)
    return jnp.where(d < max_exact, d, large)


def mlstm_chunkwise(q, k, v, i_pre, f_pre):
    bsz, s, nh, dqk = q.shape
    dv = v.shape[-1]
    L = M_CHUNK
    nc = s // L
    f32 = jnp.float32

    def heads_chunks(t):
        return t.astype(f32).reshape(bsz, nc, L, nh, t.shape[-1]).transpose(0, 3, 1, 2, 4)

    q = heads_chunks(q) * (dqk ** -0.5)
    k = heads_chunks(k)
    v = heads_chunks(v)
    ig = i_pre.astype(f32).reshape(bsz, nc, L, nh).transpose(0, 3, 1, 2)
    logf = jax.nn.log_sigmoid(f_pre.astype(f32)).reshape(bsz, nc, L, nh).transpose(0, 3, 1, 2)

    b = jnp.cumsum(logf, axis=-1)
    g = b[..., -1]
    a = g[..., None] - b + ig
    m_loc = jnp.max(a, axis=-1)
    wgt = jnp.exp(a - m_loc[..., None])
    kv_c = jnp.einsum('bhcl,bhcld,bhcle->bhcde', wgt, k, v)
    ks_c = jnp.einsum('bhcl,bhcld->bhcd', wgt, k)

    def step(carry, xs):
        c_st, n_st, m_st = carry
        kv, ks, gc, ml = xs
        m_new = jnp.maximum(gc + m_st, ml)
        decay = jnp.exp(gc + m_st - m_new)
        scale = jnp.exp(ml - m_new)
        c_new = decay[..., None, None] * c_st + scale[..., None, None] * kv
        n_new = decay[..., None] * n_st + scale[..., None] * ks
        return (c_new, n_new, m_new), (c_st, n_st, m_st)

    init = (jnp.zeros((bsz, nh, dqk, dv), f32), jnp.zeros((bsz, nh, dqk), f32), jnp.zeros((bsz, nh), f32))
    xs = (jnp.moveaxis(kv_c, 2, 0), jnp.moveaxis(ks_c, 2, 0), jnp.moveaxis(g, 2, 0), jnp.moveaxis(m_loc, 2, 0))
    _, (c_prev, n_prev, m_prev) = lax.scan(step, init, xs)
    c_prev = jnp.moveaxis(c_prev, 0, 2)
    n_prev = jnp.moveaxis(n_prev, 0, 2)
    m_prev = jnp.moveaxis(m_prev, 0, 2)

    tril = jnp.tril(jnp.ones((L, L), dtype=bool))
    dmat = b[..., :, None] - b[..., None, :] + ig[..., None, :]
    dmat = jnp.where(tril, dmat, -jnp.inf)
    inter_log = b + m_prev[..., None]
    m_t = jnp.maximum(inter_log, jnp.max(dmat, axis=-1))
    sc = jnp.einsum('bhcld,bhcsd->bhcls', q, k) * jnp.exp(dmat - m_t[..., None])
    inter_w = jnp.exp(inter_log - m_t)
    num = inter_w[..., None] * jnp.einsum('bhcld,bhcde->bhcle', q, c_prev) + jnp.einsum('bhcls,bhcse->bhcle', sc, v)
    den = inter_w * jnp.einsum('bhcld,bhcd->bhcl', q, n_prev) + jnp.sum(sc, axis=-1)
    h = num / jnp.maximum(jnp.abs(den), jnp.exp(-m_t))[..., None]
    return h.transpose(0, 2, 3, 1, 4).reshape(bsz, s, nh, dv)


def dsa_attention(q_att, k_att, v_att, q_idx, k_idx, w_idx, rel_bias):
    bsz, s = q_att.shape[0], q_att.shape[1]
    topk = min(TOPK_MAX, s // 4)
    nqb = s // Q_BLOCK
    key_pos = jnp.arange(s, dtype=jnp.int32)

    def to_blocks(t):
        return t.reshape(bsz, nqb, Q_BLOCK, *t.shape[2:]).swapaxes(0, 1)

    def one_block(args):
        qa, qi, wi, st = args
        qpos = st + jnp.arange(Q_BLOCK, dtype=jnp.int32)
        isc = jnp.einsum('bqhd,bsd->bqhs', qi, k_idx).astype(jnp.float32) * (IDX_DIM ** -0.5)
        score = jnp.einsum('bqhs,bqh->bqs', jax.nn.relu(isc), wi.astype(jnp.float32))
        score = jnp.where(key_pos[None, None, :] <= qpos[None, :, None], score, -jnp.inf)
        _, idx = lax.top_k(score, topk)
        valid = idx <= qpos[None, :, None]
        ks = jax.vmap(lambda kb, ib: kb[ib])(k_att, idx)
        vs = jax.vmap(lambda vb, ib: vb[ib])(v_att, idx)
        bias = rel_bias[rel_bucket(qpos[None, :, None] - idx)]
        logits = jnp.einsum('bqhd,bqkhd->bqhk', qa, ks).astype(jnp.float32) * (A_DH ** -0.5)
        logits = logits + bias.astype(jnp.float32).transpose(0, 1, 3, 2)
        logits = jnp.where(valid[:, :, None, :], logits, -jnp.inf)
        p = jax.nn.softmax(logits, axis=-1)
        return jnp.einsum('bqhk,bqkhd->bqhd', p.astype(vs.dtype), vs)

    starts = jnp.arange(nqb, dtype=jnp.int32) * Q_BLOCK
    out = lax.map(one_block, (to_blocks(q_att), to_blocks(q_idx), to_blocks(w_idx), starts))
    return out.swapaxes(0, 1).reshape(bsz, s, A_HEADS, A_DH)


def setup_inputs(seed: int = 0) -> dict:
    key = jax.random.key(seed)
    ks = jax.random.split(key, 20)
    f32 = jnp.float32
    nrm = lambda k, shape, scale: jax.random.normal(k, shape, f32) * scale
    return {
        'x': nrm(ks[0], (BATCH, SEQ, D_MODEL), 1.0),
        'norm_mix': 1.0 + nrm(ks[1], (DEPTH, D_MODEL), 0.01),
        'w_in': nrm(ks[2], (DEPTH, D_MODEL, D_IN), D_MODEL ** -0.5),
        'mlstm_conv_w': nrm(ks[3], (DEPTH, M_CONV, 2 * M_HEADS * M_DQK), M_CONV ** -0.5),
        'mlstm_conv_b': nrm(ks[4], (DEPTH, 2 * M_HEADS * M_DQK), 0.01),
        'i_bias': nrm(ks[5], (DEPTH, M_HEADS), 0.1),
        'f_bias': 3.0 + nrm(ks[6], (DEPTH, M_HEADS), 0.5),
        'mlstm_norm': 1.0 + nrm(ks[7], (DEPTH, M_WIDTH), 0.01),
        'idx_k_norm': 1.0 + nrm(ks[8], (DEPTH, IDX_DIM), 0.01),
        'rel_bias': nrm(ks[9], (REL_BUCKETS, A_HEADS), 0.5),
        'w_out': nrm(ks[10], (DEPTH, D_MIX, D_MODEL), D_MIX ** -0.5),
        'norm_ffn': 1.0 + nrm(ks[11], (DEPTH, D_MODEL), 0.01),
        'w_up': nrm(ks[12], (DEPTH, D_MODEL, 2 * D_FF), D_MODEL ** -0.5),
        'ffn_conv_w': nrm(ks[13], (DEPTH, FFN_CONV, 2 * D_FF), FFN_CONV ** -0.5),
        'ffn_conv_b': nrm(ks[14], (DEPTH, 2 * D_FF), 0.01),
        'w_down': nrm(ks[15], (DEPTH, D_FF, D_MODEL), D_FF ** -0.5),
        'norm_final': 1.0 + nrm(ks[16], (D_MODEL,), 0.01),
    }


def reference(x, norm_mix, w_in, mlstm_conv_w, mlstm_conv_b, i_bias, f_bias, mlstm_norm,
              idx_k_norm, rel_bias, w_out, norm_ffn, w_up, ffn_conv_w, ffn_conv_b, w_down, norm_final):
    bsz, s, _ = x.shape
    for l in range(DEPTH):
        h = rms_norm(x, norm_mix[l])
        z = h @ w_in[l]
        (m_q, m_k, m_v, m_o, m_i, m_f, a_q, a_k, a_v, i_q, i_k, i_w) = split_cols(z, IN_SIZES)

        qk = jax.nn.silu(causal_dwconv(jnp.concatenate([m_q, m_k], axis=-1), mlstm_conv_w[l], mlstm_conv_b[l]))
        m_q, m_k = qk[..., :M_HEADS * M_DQK], qk[..., M_HEADS * M_DQK:]
        hm = mlstm_chunkwise(m_q.reshape(bsz, s, M_HEADS, M_DQK), m_k.reshape(bsz, s, M_HEADS, M_DQK),
                             m_v.reshape(bsz, s, M_HEADS, M_DV), m_i + i_bias[l], m_f + f_bias[l])
        hm = rms_norm(hm, mlstm_norm[l].reshape(M_HEADS, M_DV)).astype(x.dtype)
        hm = jax.nn.sigmoid(m_o) * hm.reshape(bsz, s, M_WIDTH)

        ha = dsa_attention(a_q.reshape(bsz, s, A_HEADS, A_DH), a_k.reshape(bsz, s, A_HEADS, A_DH),
                           a_v.reshape(bsz, s, A_HEADS, A_DH), i_q.reshape(bsz, s, IDX_HEADS, IDX_DIM),
                           rms_norm(i_k, idx_k_norm[l]), i_w * (IDX_HEADS ** -0.5), rel_bias)
        ha = ha.reshape(bsz, s, A_WIDTH)

        x = x + jnp.concatenate([hm, ha], axis=-1) @ w_out[l]

        h = rms_norm(x, norm_ffn[l])
        u = causal_dwconv(h @ w_up[l], ffn_conv_w[l], ffn_conv_b[l])
        gate, val = u[..., :D_FF], u[..., D_FF:]
        x = x + (jax.nn.silu(gate) * val) @ w_down[l]
    return rms_norm(x, norm_final)
```

```python
import functools
import math

import jax
import jax.numpy as jnp
from jax import lax
from jax.experimental import pallas as pl
from jax.experimental.pallas import tpu as pltpu

F32 = jnp.float32
BF16 = jnp.bfloat16

EPS = 1e-6
LANES = 128
SUBLANES = 8

M_HEADS, M_DV, M_DQK, M_CONV = 4, 128, 64, 4
A_HEADS, A_DH = 8, 64
IDX_HEADS, IDX_DIM = 8, 64
TOPK_MAX = 256
REL_BUCKETS, REL_MAX_DIST = 32, 128
FFN_CONV = 3
M_WIDTH = M_HEADS * M_DV
A_WIDTH = A_HEADS * A_DH
QK_WIDTH = 2 * M_HEADS * M_DQK

ROW_TILE = 512
M_CHUNK = 256
Q_TILE = 128
K_CHUNK = 512
K_PAD = K_CHUNK - Q_TILE
FF_CHUNK = 256
FFN_HALO = 16
BISECT_ITERS = 22
NEG_BIG = -1e30
VMEM_LIMIT = 56 * 1024 * 1024

_NT = (((1,), (1,)), ((), ()))
_TN = (((0,), (0,)), ((), ()))


def _rms(x, g):
    return x * lax.rsqrt(jnp.mean(x * x, axis=-1, keepdims=True) + EPS) * g


def _sigmoid(x):
    return 1.0 / (1.0 + jnp.exp(-x))


def _log_sigmoid(x):
    return jnp.minimum(x, 0.0) - jnp.log(1.0 + jnp.exp(-jnp.abs(x)))


def _in_proj_kernel(x_ref, g_ref, w_ref, gk_ref,
                    qk_ref, mv_ref, mo_ref, aq_ref, ak_ref, av_ref, iq_ref, kn_ref, gw_ref):
    h = _rms(x_ref[...], g_ref[...]).astype(BF16)

    def proj(col, width):
        return jnp.dot(h, w_ref[:, col:col + width], preferred_element_type=F32)

    qk_ref[...] = proj(0, 512)
    mv_ref[...] = proj(512, 512).astype(BF16)
    mo_ref[...] = proj(1024, 512)
    aq_ref[...] = proj(1536, 512).astype(BF16)
    ak_ref[...] = proj(2048, 512).astype(BF16)
    av_ref[...] = proj(2560, 512).astype(BF16)
    iq_ref[...] = proj(3072, 512).astype(BF16)
    kn_ref[...] = _rms(proj(3584, LANES), gk_ref[...]).astype(BF16)
    gw_ref[...] = proj(3712, LANES)


def _in_proj(x2, g, w, gk):
    t = x2.shape[0]
    d = x2.shape[1]
    row = lambda i: (i, 0)
    fixed = lambda i: (0, 0)
    wide = lambda dt: jax.ShapeDtypeStruct((t, 512), dt)
    return pl.pallas_call(
        _in_proj_kernel,
        grid=(t // ROW_TILE,),
        in_specs=[pl.BlockSpec((ROW_TILE, d), row), pl.BlockSpec((1, d), fixed),
                  pl.BlockSpec(w.shape, fixed), pl.BlockSpec((1, LANES), fixed)],
        out_specs=[pl.BlockSpec((ROW_TILE, 512), row)] * 7 + [pl.BlockSpec((ROW_TILE, LANES), row)] * 2,
        out_shape=[wide(F32), wide(BF16), wide(F32), wide(BF16), wide(BF16), wide(BF16), wide(BF16),
                   jax.ShapeDtypeStruct((t, LANES), BF16), jax.ShapeDtypeStruct((t, LANES), F32)],
        compiler_params=pltpu.CompilerParams(dimension_semantics=("parallel",), vmem_limit_bytes=VMEM_LIMIT),
        name="in_proj",
    )(x2, g, w, gk)


def _mlstm_kernel(qk_ref, v_ref, o_ref, gw_ref, cw_ref, cb_ref, gb_ref, gn_ref,
                  out_ref, xp_ref, c_ref, n_ref, m_ref):
    L = M_CHUNK

    @pl.when(pl.program_id(1) == 0)
    def _():
        xp_ref[0:SUBLANES, :] = jnp.zeros((SUBLANES, QK_WIDTH), F32)
        c_ref[...] = jnp.zeros_like(c_ref)
        n_ref[...] = jnp.zeros_like(n_ref)
        m_ref[...] = jnp.zeros_like(m_ref)

    x = qk_ref[...]
    xp_ref[SUBLANES:, :] = x
    y = cb_ref[...] + x * cw_ref[M_CONV - 1:M_CONV, :]
    for j in range(M_CONV - 1):
        shift = M_CONV - 1 - j
        y = y + xp_ref[SUBLANES - shift:SUBLANES - shift + L, :] * cw_ref[j:j + 1, :]
    xp_ref[0:SUBLANES, :] = x[L - SUBLANES:, :]
    y = y * _sigmoid(y)
    q_all = y[:, :M_HEADS * M_DQK] * (M_DQK ** -0.5)
    k_all = y[:, M_HEADS * M_DQK:]

    gw = gw_ref[...] + gb_ref[...]
    lane = lax.broadcasted_iota(jnp.int32, (L, LANES), 1)
    logf = _log_sigmoid(gw)
    r_i = lax.broadcasted_iota(jnp.int32, (L, L), 0)
    c_i = lax.broadcasted_iota(jnp.int32, (L, L), 1)
    tril = r_i >= c_i
    tril_b = tril.astype(BF16)
    hi = logf.astype(BF16)
    rem = logf - hi.astype(F32)
    mid = rem.astype(BF16)
    lo = (rem - mid.astype(F32)).astype(BF16)
    bcum = (jnp.dot(tril_b, hi, preferred_element_type=F32) + jnp.dot(tril_b, mid, preferred_element_type=F32)
            + jnp.dot(tril_b, lo, preferred_element_type=F32))
    cols = jnp.where(lane < M_HEADS, gw, bcum)
    rows = cols.T

    half = lax.broadcasted_iota(jnp.int32, (L, LANES), 1) < M_DQK
    for hd in range(M_HEADS):
        pair, odd = hd // 2, hd % 2
        sel = half if odd == 0 else jnp.logical_not(half)
        qp = jnp.where(sel, q_all[:, pair * LANES:(pair + 1) * LANES], 0.0)
        kp = k_all[:, pair * LANES:(pair + 1) * LANES]
        vh = v_ref[:, hd * M_DV:(hd + 1) * M_DV]
        ig_col = cols[:, hd:hd + 1]
        b_col = cols[:, M_HEADS + hd:M_HEADS + hd + 1]
        ig_row = rows[hd:hd + 1, :]
        b_row = rows[M_HEADS + hd:M_HEADS + hd + 1, :]
        c_prev = c_ref[hd]
        n_prev = n_ref[hd]
        m_prev = m_ref[hd][:, 0:1]

        dmat = jnp.where(tril, b_col - b_row + ig_row, -jnp.inf)
        inter_log = b_col + m_prev
        m_t = jnp.maximum(inter_log, jnp.max(dmat, axis=-1, keepdims=True))
        qb = qp.astype(BF16)
        sc = lax.dot_general(qb, kp.astype(BF16), _NT, preferred_element_type=F32) * jnp.exp(dmat - m_t)
        inter_w = jnp.exp(inter_log - m_t)
        num = (inter_w * jnp.dot(qb, c_prev.astype(BF16), preferred_element_type=F32)
               + jnp.dot(sc.astype(BF16), vh, preferred_element_type=F32))
        den = inter_w * jnp.sum(qp * n_prev, axis=-1, keepdims=True) + jnp.sum(sc, axis=-1, keepdims=True)
        hh = num / jnp.maximum(jnp.abs(den), jnp.exp(-m_t))

        g = b_col[L - 1:L, :]
        a_col = g - b_col + ig_col
        m_loc = jnp.max(a_col, axis=0, keepdims=True)
        kw = kp * jnp.exp(a_col - m_loc)
        kv = lax.dot_general(kw.astype(BF16), vh, _TN, preferred_element_type=F32)
        ks = jnp.sum(kw, axis=0, keepdims=True)
        m_new = jnp.maximum(g + m_prev, m_loc)
        decay = jnp.exp(g + m_prev - m_new)
        scale = jnp.exp(m_loc - m_new)
        c_ref[hd] = decay * c_prev + scale * kv
        n_ref[hd] = decay * n_prev + scale * ks
        m_ref[hd] = jnp.broadcast_to(m_new, (1, LANES))

        hn = _rms(hh, gn_ref[:, hd * M_DV:(hd + 1) * M_DV])
        out_ref[:, hd * M_DV:(hd + 1) * M_DV] = (_sigmoid(o_ref[:, hd * M_DV:(hd + 1) * M_DV]) * hn).astype(BF16)


def _mlstm(qk, mv, mo, gw, cw, cb, gb, gn):
    bsz, s, _ = qk.shape
    L = M_CHUNK
    blk = lambda width: pl.BlockSpec((None, L, width), lambda b, c: (b, c, 0))
    fixed = lambda shape: pl.BlockSpec(shape, lambda b, c: (0, 0))
    return pl.pallas_call(
        _mlstm_kernel,
        grid=(bsz, s // L),
        in_specs=[blk(QK_WIDTH), blk(M_WIDTH), blk(M_WIDTH), blk(LANES),
                  fixed((M_CONV, QK_WIDTH)), fixed((1, QK_WIDTH)), fixed((1, LANES)), fixed((1, M_WIDTH))],
        out_specs=blk(M_WIDTH),
        out_shape=jax.ShapeDtypeStruct((bsz, s, M_WIDTH), BF16),
        scratch_shapes=[pltpu.VMEM((L + SUBLANES, QK_WIDTH), F32),
                        pltpu.VMEM((M_HEADS, LANES, M_DV), F32),
                        pltpu.VMEM((M_HEADS, 1, LANES), F32),
                        pltpu.VMEM((M_HEADS, 1, LANES), F32)],
        compiler_params=pltpu.CompilerParams(dimension_semantics=("parallel", "arbitrary"),
                                             vmem_limit_bytes=VMEM_LIMIT),
        name="mlstm",
    )(qk, mv, mo, gw, cw, cb, gb, gn)


def _lane_fold(x, op):
    acc = x[:, 0:LANES]
    for j in range(1, x.shape[1] // LANES):
        acc = op(acc, x[:, j * LANES:(j + 1) * LANES])
    return acc


def _dsa_kernel(iq_ref, gw_ref, aq_ref, kn_ref, ak_ref, av_ref, tb_ref, out_ref, score_ref, *, topk):
    i = pl.program_id(1)
    n_chunks = i // (K_CHUNK // Q_TILE) + 1
    t0 = i * Q_TILE
    kf = float(topk)

    def key_start(c):
        return pl.multiple_of((i - (K_CHUNK // Q_TILE) * c) * Q_TILE, Q_TILE)

    lane512 = lax.broadcasted_iota(jnp.int32, (Q_TILE, A_WIDTH), 1)
    low_half = (lane512 % LANES) < A_DH
    row_i = lax.broadcasted_iota(jnp.int32, (Q_TILE, K_CHUNK), 0)
    col_i = lax.broadcasted_iota(jnp.int32, (Q_TILE, K_CHUNK), 1)

    iq = iq_ref[...]
    iq_even = jnp.where(low_half, iq, jnp.zeros_like(iq))
    iq_odd = jnp.where(low_half, jnp.zeros_like(iq), iq)
    w_idx = gw_ref[:, 2 * M_HEADS:2 * M_HEADS + IDX_HEADS] * (IDX_HEADS ** -0.5)

    def score_chunk(c, carry):
        mx, mn = carry
        start = key_start(c)
        kn = kn_ref[pl.ds(start, K_CHUNK), :]
        acc = jnp.zeros((Q_TILE, K_CHUNK), F32)
        for hd in range(IDX_HEADS):
            src = iq_even if hd % 2 == 0 else iq_odd
            qh = src[:, (hd // 2) * LANES:(hd // 2 + 1) * LANES]
            isc = lax.dot_general(qh, kn, _NT, preferred_element_type=F32)
            acc = acc + jnp.maximum(isc, 0.0) * w_idx[:, hd:hd + 1]
        key = start - K_PAD + col_i
        ok = jnp.logical_and(key >= 0, key <= t0 + row_i)
        score_ref[:, pl.ds(pl.multiple_of(c * K_CHUNK, K_CHUNK), K_CHUNK)] = jnp.where(ok, acc, -jnp.inf)
        mx = jnp.maximum(mx, _lane_fold(jnp.where(ok, acc, -jnp.inf), jnp.maximum))
        mn = jnp.minimum(mn, _lane_fold(jnp.where(ok, acc, jnp.inf), jnp.minimum))
        return mx, mn

    mx, mn = lax.fori_loop(0, n_chunks, score_chunk,
                           (jnp.full((Q_TILE, LANES), -jnp.inf, F32), jnp.full((Q_TILE, LANES), jnp.inf, F32)))
    row_max = jnp.max(mx, axis=-1, keepdims=True)
    row_min = jnp.min(mn, axis=-1, keepdims=True)

    def chunk_of(c):
        return score_ref[:, pl.ds(pl.multiple_of(c * K_CHUNK, K_CHUNK), K_CHUNK)]

    def count_ge(thr):
        def body(c, cnt):
            return cnt + _lane_fold(jnp.where(chunk_of(c) >= thr, 1.0, 0.0), jnp.add)
        cnt = lax.fori_loop(0, n_chunks, body, jnp.zeros((Q_TILE, LANES), F32))
        return jnp.sum(cnt, axis=-1, keepdims=True)

    def count_gt(thr):
        def body(c, cnt):
            return cnt + _lane_fold(jnp.where(chunk_of(c) > thr, 1.0, 0.0), jnp.add)
        cnt = lax.fori_loop(0, n_chunks, body, jnp.zeros((Q_TILE, LANES), F32))
        return jnp.sum(cnt, axis=-1, keepdims=True)

    def max_below(bound):
        def body(c, best):
            x = chunk_of(c)
            return jnp.maximum(best, _lane_fold(jnp.where(x < bound, x, -jnp.inf), jnp.maximum))
        best = lax.fori_loop(0, n_chunks, body, jnp.full((Q_TILE, LANES), -jnp.inf, F32))
        return jnp.max(best, axis=-1, keepdims=True)

    t_col = t0 + lax.broadcasted_iota(jnp.int32, (Q_TILE, 1), 0)
    active = (t_col + 1) > topk
    floor = jnp.full((Q_TILE, 1), jnp.finfo(jnp.float32).min, F32)

    def search():
        def bisect(_, carry):
            lo, hi, hi_num = carry
            mid = 0.5 * (lo + hi_num)
            ge = count_ge(mid) >= kf
            return jnp.where(ge, mid, lo), jnp.where(ge, hi, mid), jnp.where(ge, hi_num, mid)

        lo, hi, _ = lax.fori_loop(0, BISECT_ITERS, bisect,
                                  (row_min, jnp.full((Q_TILE, 1), jnp.inf, F32), row_max))

        def unfinished(thr_cnt):
            thr, cnt = thr_cnt
            return jnp.max(jnp.where(jnp.logical_and(active, cnt < kf), 1, 0)) > 0

        def cond(carry):
            return carry[3]

        def step(carry):
            hi, thr, cnt, _ = carry
            hi = jnp.where(cnt < kf, thr, hi)
            thr = max_below(hi)
            cnt = count_ge(thr)
            return hi, thr, cnt, unfinished((thr, cnt))

        thr0 = max_below(hi)
        cnt0 = count_ge(thr0)
        _, thr, cnt, _ = lax.while_loop(cond, step, (hi, thr0, cnt0, unfinished((thr0, cnt0))))
        thr = jnp.where(active, thr, floor)

        excess = jnp.logical_and(active, cnt > kf)

        @pl.when(jnp.max(jnp.where(excess, 1, 0)) > 0)
        def _():
            need = kf - count_gt(thr)

            def eq_below(pos):
                def body(c, cnt):
                    key = (key_start(c) - K_PAD + col_i).astype(F32)
                    hit = jnp.logical_and(chunk_of(c) == thr, key < pos)
                    return cnt + _lane_fold(jnp.where(hit, 1.0, 0.0), jnp.add)
                cnt = lax.fori_loop(0, n_chunks, body, jnp.zeros((Q_TILE, LANES), F32))
                return jnp.sum(cnt, axis=-1, keepdims=True)

            def pos_search(_, carry):
                lo_p, hi_p = carry
                mid_p = jnp.floor(0.5 * (lo_p + hi_p))
                ge = eq_below(mid_p) >= need
                return jnp.where(ge, lo_p, mid_p + 1.0), jnp.where(ge, mid_p, hi_p)

            s_total = float(score_ref.shape[1])
            n_pos = int(math.ceil(math.log2(s_total + 1))) + 1
            _, pos = lax.fori_loop(0, n_pos, pos_search,
                                   (jnp.zeros((Q_TILE, 1), F32), jnp.full((Q_TILE, 1), s_total, F32)))
            pos = jnp.where(excess, pos, s_total)

            def demote(c, _):
                key = (key_start(c) - K_PAD + col_i).astype(F32)
                x = chunk_of(c)
                drop = jnp.logical_and(x == thr, key >= pos)
                score_ref[:, pl.ds(pl.multiple_of(c * K_CHUNK, K_CHUNK), K_CHUNK)] = jnp.where(drop, -jnp.inf, x)
                return 0

            lax.fori_loop(0, n_chunks, demote, 0)

        return thr

    thr = lax.cond(t0 + Q_TILE > topk, search, lambda: floor)

    aq = aq_ref[...]
    aq_even = jnp.where(low_half, aq, jnp.zeros_like(aq))
    aq_odd = jnp.where(low_half, jnp.zeros_like(aq), aq)
    low128 = lax.broadcasted_iota(jnp.int32, (Q_TILE, LANES), 1) < A_DH
    near = K_CHUNK - 2 * Q_TILE

    for pair in range(A_HEADS // 2):
        cs = slice(pair * LANES, (pair + 1) * LANES)
        q_e, q_o = aq_even[:, cs], aq_odd[:, cs]

        def attend(c, carry, with_bias, pair=pair, cs=cs, q_e=q_e, q_o=q_o):
            start = key_start(c)
            kp = ak_ref[pl.ds(start, K_CHUNK), cs]
            vp = av_ref[pl.ds(start, K_CHUNK), cs]
            madd = jnp.where(chunk_of(c) >= thr, 0.0, NEG_BIG)
            out = []
            for e, qh in enumerate((q_e, q_o)):
                m_old, l_old, acc = carry[e]
                s = lax.dot_general(qh, kp, _NT, preferred_element_type=F32) + madd
                if with_bias:
                    s = jnp.concatenate([s[:, :near], s[:, near:] + tb_ref[2 * pair + e]], axis=1)
                m_new = jnp.maximum(m_old, jnp.max(s, axis=-1, keepdims=True))
                alpha = jnp.exp(m_old - m_new)
                p = jnp.exp(s - m_new)
                l_new = alpha * l_old + jnp.sum(p, axis=-1, keepdims=True)
                acc = alpha * acc + jnp.dot(p.astype(BF16), vp, preferred_element_type=F32)
                out.append((m_new, l_new, acc))
            return tuple(out)

        init = tuple((jnp.full((Q_TILE, 1), NEG_BIG, F32), jnp.zeros((Q_TILE, 1), F32),
                      jnp.zeros((Q_TILE, LANES), F32)) for _ in range(2))
        state = attend(0, init, True)
        state = lax.fori_loop(1, n_chunks, functools.partial(attend, with_bias=False), state)
        (_, l_e, acc_e), (_, l_o, acc_o) = state
        out_ref[:, cs] = jnp.where(low128, acc_e / l_e, acc_o / l_o).astype(BF16)


def _dsa(iq, gw, aq, kn, ak, av, tb, topk):
    bsz, s, _ = iq.shape
    sp = s + K_PAD
    qblk = lambda width: pl.BlockSpec((None, Q_TILE, width), lambda b, i: (b, i, 0))
    kblk = lambda width: pl.BlockSpec((None, sp, width), lambda b, i: (b, 0, 0))
    return pl.pallas_call(
        functools.partial(_dsa_kernel, topk=topk),
        grid=(bsz, s // Q_TILE),
        in_specs=[qblk(A_WIDTH), qblk(LANES), qblk(A_WIDTH), kblk(LANES), kblk(A_WIDTH), kblk(A_WIDTH),
                  pl.BlockSpec(tb.shape, lambda b, i: (0, 0, 0))],
        out_specs=qblk(A_WIDTH),
        out_shape=jax.ShapeDtypeStruct((bsz, s, A_WIDTH), BF16),
        scratch_shapes=[pltpu.VMEM((Q_TILE, max(s, K_CHUNK)), F32)],
        compiler_params=pltpu.CompilerParams(dimension_semantics=("parallel", "arbitrary"),
                                             vmem_limit_bytes=VMEM_LIMIT),
        name="dsa",
    )(iq, gw, aq, kn, ak, av, tb)


def _rel_bias_table(rel_bias):
    a = jnp.arange(Q_TILE, dtype=jnp.int32)[:, None]
    c = jnp.arange(2 * Q_TILE, dtype=jnp.int32)[None, :]
    dist = Q_TILE + a - c
    max_exact = REL_BUCKETS // 2
    d = jnp.maximum(dist, 0)
    large = max_exact + (jnp.log(jnp.maximum(d, 1).astype(F32) / max_exact)
                         / math.log(REL_MAX_DIST / max_exact) * (REL_BUCKETS - max_exact)).astype(jnp.int32)
    large = jnp.minimum(large, REL_BUCKETS - 1)
    bucket = jnp.where(d < max_exact, d, large)
    table = rel_bias.astype(F32)[bucket]
    far = rel_bias.astype(F32)[REL_BUCKETS - 1]
    return jnp.transpose(table - far, (2, 0, 1))


def _out_proj_kernel(x_ref, hm_ref, ha_ref, w_ref, g_ref, x1_ref, h2_ref):
    x1 = (x_ref[...] + jnp.dot(hm_ref[...], w_ref[0:M_WIDTH, :], preferred_element_type=F32)
          + jnp.dot(ha_ref[...], w_ref[M_WIDTH:, :], preferred_element_type=F32))
    x1_ref[...] = x1
    h2_ref[...] = _rms(x1, g_ref[...]).astype(BF16)


def _out_proj(x2, hm, ha, w, g):
    t, d = x2.shape
    row = lambda i: (i, 0)
    fixed = lambda i: (0, 0)
    return pl.pallas_call(
        _out_proj_kernel,
        grid=(t // ROW_TILE,),
        in_specs=[pl.BlockSpec((ROW_TILE, d), row), pl.BlockSpec((ROW_TILE, M_WIDTH), row),
                  pl.BlockSpec((ROW_TILE, A_WIDTH), row), pl.BlockSpec(w.shape, fixed), pl.BlockSpec((1, d), fixed)],
        out_specs=[pl.BlockSpec((ROW_TILE, d), row), pl.BlockSpec((ROW_TILE, d), row)],
        out_shape=[jax.ShapeDtypeStruct((t, d), F32), jax.ShapeDtypeStruct((t, d), BF16)],
        compiler_params=pltpu.CompilerParams(dimension_semantics=("parallel",), vmem_limit_bytes=VMEM_LIMIT),
        name="out_proj",
    )(x2, hm, ha, w, g)


def _ffn_kernel(x1_ref, h_ref, halo_ref, wg_ref, wv_ref, cwg_ref, cwv_ref, cbg_ref, cbv_ref, wd_ref, g_ref,
                out_ref, hext_ref, u_ref, a_ref, *, tiles_per_seq):
    tm = ROW_TILE
    first = (pl.program_id(0) % tiles_per_seq) == 0
    hext_ref[0:FFN_HALO, :] = jnp.where(first, jnp.zeros_like(halo_ref[...]), halo_ref[...])
    hext_ref[FFN_HALO:, :] = h_ref[...]
    hext = hext_ref[...]
    n_ff = wg_ref.shape[0]

    def conv(w_ref, cw_ref, cb_ref, c):
        u_ref[...] = jnp.dot(hext, w_ref[c], preferred_element_type=F32)
        y = cb_ref[c] + u_ref[FFN_HALO:, :] * cw_ref[c, FFN_CONV - 1:FFN_CONV, :]
        for j in range(FFN_CONV - 1):
            shift = FFN_CONV - 1 - j
            y = y + u_ref[FFN_HALO - shift:FFN_HALO - shift + tm, :] * cw_ref[c, j:j + 1, :]
        return y

    for c in range(n_ff):
        gate = conv(wg_ref, cwg_ref, cbg_ref, c)
        val = conv(wv_ref, cwv_ref, cbv_ref, c)
        a_ref[:, c * FF_CHUNK:(c + 1) * FF_CHUNK] = (gate * _sigmoid(gate) * val).astype(BF16)

    x2 = x1_ref[...] + jnp.dot(a_ref[...], wd_ref[...], preferred_element_type=F32)
    out_ref[...] = _rms(x2, g_ref[...])


def _ffn(x1, h2, wg, wv, cwg, cwv, cbg, cbv, wd, g, tiles_per_seq):
    t, d = x1.shape
    n_ff = wg.shape[0]
    row = lambda i: (i, 0)
    halo = lambda i: (jnp.maximum(i * (ROW_TILE // FFN_HALO) - 1, 0), 0)
    fixed2 = lambda i: (0, 0)
    fixed3 = lambda i: (0, 0, 0)
    return pl.pallas_call(
        functools.partial(_ffn_kernel, tiles_per_seq=tiles_per_seq),
        grid=(t // ROW_TILE,),
        in_specs=[pl.BlockSpec((ROW_TILE, d), row), pl.BlockSpec((ROW_TILE, d), row),
                  pl.BlockSpec((FFN_HALO, d), halo),
                  pl.BlockSpec(wg.shape, fixed3), pl.BlockSpec(wv.shape, fixed3),
                  pl.BlockSpec(cwg.shape, fixed3), pl.BlockSpec(cwv.shape, fixed3),
                  pl.BlockSpec(cbg.shape, fixed3), pl.BlockSpec(cbv.shape, fixed3),
                  pl.BlockSpec(wd.shape, fixed2), pl.BlockSpec((1, d), fixed2)],
        out_specs=pl.BlockSpec((ROW_TILE, d), row),
        out_shape=jax.ShapeDtypeStruct((t, d), F32),
        scratch_shapes=[pltpu.VMEM((ROW_TILE + FFN_HALO, d), BF16),
                        pltpu.VMEM((ROW_TILE + FFN_HALO, FF_CHUNK), F32),
                        pltpu.VMEM((ROW_TILE, n_ff * FF_CHUNK), BF16)],
        compiler_params=pltpu.CompilerParams(dimension_semantics=("parallel",), vmem_limit_bytes=VMEM_LIMIT),
        name="ffn",
    )(x1, h2, h2, wg, wv, cwg, cwv, cbg, cbv, wd, g)


def _forward(x, norm_mix, w_in, conv_w, conv_b, i_bias, f_bias, mlstm_norm, idx_k_norm, rel_bias,
             w_out, norm_ffn, w_up, ffn_conv_w, ffn_conv_b, w_down, norm_final):
    bsz, s, d = x.shape
    t = bsz * s
    assert t % ROW_TILE == 0 and s % ROW_TILE == 0 and s % M_CHUNK == 0 and s % K_CHUNK == 0
    topk = min(TOPK_MAX, s // 4)

    sizes = (M_HEADS * M_DQK, M_HEADS * M_DQK, M_WIDTH, M_WIDTH, M_HEADS, M_HEADS,
             A_WIDTH, A_WIDTH, A_WIDTH, IDX_HEADS * IDX_DIM, IDX_DIM, IDX_HEADS)
    offs = [0]
    for sz in sizes:
        offs.append(offs[-1] + sz)
    col = lambda k: w_in[:, offs[k]:offs[k + 1]]
    small_pad = jnp.zeros((d, LANES - 2 * M_HEADS - IDX_HEADS), w_in.dtype)
    w_cat = jnp.concatenate([col(0), col(1), col(2), col(3), col(6) * (A_DH ** -0.5), col(7), col(8),
                             col(9) * (IDX_DIM ** -0.5), col(10), col(10), col(4), col(5), col(11), small_pad],
                            axis=1).astype(BF16)
    gk = jnp.concatenate([idx_k_norm, idx_k_norm]).reshape(1, LANES).astype(F32)

    x2 = x.reshape(t, d)
    qk, mv, mo, aq, ak, av, iq, kn, gw = _in_proj(x2, norm_mix.reshape(1, d).astype(F32), w_cat, gk)

    per_b = lambda a: a.reshape(bsz, s, a.shape[-1])
    gate_bias = jnp.concatenate([i_bias, f_bias, jnp.zeros((LANES - 2 * M_HEADS,), F32)]).reshape(1, LANES)
    hm = _mlstm(per_b(qk), per_b(mv), per_b(mo), per_b(gw), conv_w.astype(F32), conv_b.reshape(1, -1).astype(F32),
                gate_bias, mlstm_norm.reshape(1, -1).astype(F32))

    front = lambda a: jnp.pad(per_b(a), ((0, 0), (K_PAD, 0), (0, 0)))
    ha = _dsa(per_b(iq), per_b(gw), per_b(aq), front(kn), front(ak), front(av), _rel_bias_table(rel_bias), topk)

    x1, h2 = _out_proj(x2, hm.reshape(t, M_WIDTH), ha.reshape(t, A_WIDTH), w_out.astype(BF16),
                       norm_ffn.reshape(1, d).astype(F32))

    d_ff = w_down.shape[0]
    n_ff = d_ff // FF_CHUNK
    assert d_ff % FF_CHUNK == 0
    chunks = lambda w: jnp.transpose(w.reshape(w.shape[0], n_ff, FF_CHUNK), (1, 0, 2))
    wg = chunks(w_up[:, :d_ff]).astype(BF16)
    wv = chunks(w_up[:, d_ff:]).astype(BF16)
    cwg = chunks(ffn_conv_w[:, :d_ff]).astype(F32)
    cwv = chunks(ffn_conv_w[:, d_ff:]).astype(F32)
    cbg = ffn_conv_b[:d_ff].reshape(n_ff, 1, FF_CHUNK).astype(F32)
    cbv = ffn_conv_b[d_ff:].reshape(n_ff, 1, FF_CHUNK).astype(F32)
    out = _ffn(x1, h2, wg, wv, cwg, cwv, cbg, cbv, w_down.astype(BF16), norm_final.reshape(1, d).astype(F32),
               s // ROW_TILE)
    return hm, ha, x1.reshape(bsz, s, d), out.reshape(bsz, s, d)


def kernel(x, norm_mix, w_in, mlstm_conv_w, mlstm_conv_b, i_bias, f_bias, mlstm_norm, idx_k_norm, rel_bias,
           w_out, norm_ffn, w_up, ffn_conv_w, ffn_conv_b, w_down, norm_final):
    assert norm_mix.shape[0] == 1, "single-layer operation"
    return _forward(x, norm_mix[0], w_in[0], mlstm_conv_w[0], mlstm_conv_b[0], i_bias[0], f_bias[0],
                    mlstm_norm[0], idx_k_norm[0], rel_bias, w_out[0], norm_ffn[0], w_up[0], ffn_conv_w[0],
                    ffn_conv_b[0], w_down[0], norm_final)[3]
```

```python
import functools
import math

import jax
import jax.numpy as jnp
from jax import lax
from jax.experimental import pallas as pl
from jax.experimental.pallas import tpu as pltpu

F32 = jnp.float32
BF16 = jnp.bfloat16

EPS = 1e-6
LANES = 128
SUBLANES = 8

M_HEADS, M_DV, M_DQK, M_CONV = 4, 128, 64, 4
A_HEADS, A_DH = 8, 64
IDX_HEADS, IDX_DIM = 8, 64
TOPK_MAX = 256
REL_BUCKETS, REL_MAX_DIST = 32, 128
FFN_CONV = 3
M_WIDTH = M_HEADS * M_DV
A_WIDTH = A_HEADS * A_DH
QK_WIDTH = 2 * M_HEADS * M_DQK
LOG2E = math.log2(math.e)

ROW_TILE = 512
M_CHUNK = 256
Q_TILE = 128
K_CHUNK = 512
K_PAD = K_CHUNK - Q_TILE
FOLD = 4 * SUBLANES
FF_CHUNK = 256
FFN_HALO = 16
BISECT_MIN = 16
BISECT_MAX = 30
NEG_BIG = -1e30
VMEM_LIMIT = 56 * 1024 * 1024

_NT = (((1,), (1,)), ((), ()))
_TN = (((0,), (0,)), ((), ()))


def _rms(x, g):
    return x * lax.rsqrt(jnp.mean(x * x, axis=-1, keepdims=True) + EPS) * g


def _sigmoid(x):
    return 1.0 / (1.0 + jnp.exp(-x))


def _log_sigmoid(x):
    return jnp.minimum(x, 0.0) - jnp.log(1.0 + jnp.exp(-jnp.abs(x)))


def _in_proj_kernel(x_ref, g_ref, w_ref, gk_ref,
                    qk_ref, mv_ref, mo_ref, aq_ref, ak_ref, av_ref, iq_ref, kn_ref, gw_ref):
    h = _rms(x_ref[...], g_ref[...]).astype(BF16)

    def proj(col, width):
        return jnp.dot(h, w_ref[:, col:col + width], preferred_element_type=F32)

    qk_ref[...] = proj(0, 512)
    mv_ref[...] = proj(512, 512).astype(BF16)
    mo_ref[...] = proj(1024, 512)
    aq_ref[...] = proj(1536, 512).astype(BF16)
    ak_ref[...] = proj(2048, 512).astype(BF16)
    av_ref[...] = proj(2560, 512).astype(BF16)
    iq_ref[...] = proj(3072, 512).astype(BF16)
    kn_ref[...] = _rms(proj(3584, LANES), gk_ref[...]).astype(BF16)
    gw_ref[...] = proj(3712, LANES)


def _in_proj(x2, g, w, gk):
    t = x2.shape[0]
    d = x2.shape[1]
    row = lambda i: (i, 0)
    fixed = lambda i: (0, 0)
    wide = lambda dt: jax.ShapeDtypeStruct((t, 512), dt)
    return pl.pallas_call(
        _in_proj_kernel,
        grid=(t // ROW_TILE,),
        in_specs=[pl.BlockSpec((ROW_TILE, d), row), pl.BlockSpec((1, d), fixed),
                  pl.BlockSpec(w.shape, fixed), pl.BlockSpec((1, LANES), fixed)],
        out_specs=[pl.BlockSpec((ROW_TILE, 512), row)] * 7 + [pl.BlockSpec((ROW_TILE, LANES), row)] * 2,
        out_shape=[wide(F32), wide(BF16), wide(F32), wide(BF16), wide(BF16), wide(BF16), wide(BF16),
                   jax.ShapeDtypeStruct((t, LANES), BF16), jax.ShapeDtypeStruct((t, LANES), F32)],
        compiler_params=pltpu.CompilerParams(dimension_semantics=("parallel",), vmem_limit_bytes=VMEM_LIMIT),
        name="in_proj",
    )(x2, g, w, gk)


def _mlstm_kernel(qk_ref, v_ref, o_ref, gw_ref, cw_ref, cb_ref, gb_ref, gn_ref,
                  out_ref, xp_ref, c_ref, n_ref, m_ref):
    L = M_CHUNK

    @pl.when(pl.program_id(1) == 0)
    def _():
        xp_ref[0:SUBLANES, :] = jnp.zeros((SUBLANES, QK_WIDTH), F32)
        c_ref[...] = jnp.zeros_like(c_ref)
        n_ref[...] = jnp.zeros_like(n_ref)
        m_ref[...] = jnp.zeros_like(m_ref)

    x = qk_ref[...]
    xp_ref[SUBLANES:, :] = x
    y = cb_ref[...] + x * cw_ref[M_CONV - 1:M_CONV, :]
    for j in range(M_CONV - 1):
        shift = M_CONV - 1 - j
        y = y + xp_ref[SUBLANES - shift:SUBLANES - shift + L, :] * cw_ref[j:j + 1, :]
    xp_ref[0:SUBLANES, :] = x[L - SUBLANES:, :]
    y = y * _sigmoid(y)
    q_all = y[:, :M_HEADS * M_DQK] * (M_DQK ** -0.5)
    k_all = y[:, M_HEADS * M_DQK:]

    gw = gw_ref[...] + gb_ref[...]
    lane = lax.broadcasted_iota(jnp.int32, (L, LANES), 1)
    logf = _log_sigmoid(gw)
    r_i = lax.broadcasted_iota(jnp.int32, (L, L), 0)
    c_i = lax.broadcasted_iota(jnp.int32, (L, L), 1)
    tril = r_i >= c_i
    tril_b = tril.astype(BF16)
    hi = logf.astype(BF16)
    rem = logf - hi.astype(F32)
    mid = rem.astype(BF16)
    lo = (rem - mid.astype(F32)).astype(BF16)
    bcum = (jnp.dot(tril_b, hi, preferred_element_type=F32) + jnp.dot(tril_b, mid, preferred_element_type=F32)
            + jnp.dot(tril_b, lo, preferred_element_type=F32))
    cols = jnp.where(lane < M_HEADS, gw, bcum)
    rows = cols.T

    half = lax.broadcasted_iota(jnp.int32, (L, LANES), 1) < M_DQK
    for hd in range(M_HEADS):
        pair, odd = hd // 2, hd % 2
        sel = half if odd == 0 else jnp.logical_not(half)
        qp = jnp.where(sel, q_all[:, pair * LANES:(pair + 1) * LANES], 0.0)
        kp = k_all[:, pair * LANES:(pair + 1) * LANES]
        vh = v_ref[:, hd * M_DV:(hd + 1) * M_DV]
        ig_col = cols[:, hd:hd + 1]
        b_col = cols[:, M_HEADS + hd:M_HEADS + hd + 1]
        ig_row = rows[hd:hd + 1, :]
        b_row = rows[M_HEADS + hd:M_HEADS + hd + 1, :]
        c_prev = c_ref[hd]
        n_prev = n_ref[hd]
        m_prev = m_ref[hd][:, 0:1]

        dmat = jnp.where(tril, b_col - b_row + ig_row, -jnp.inf)
        inter_log = b_col + m_prev
        m_t = jnp.maximum(inter_log, jnp.max(dmat, axis=-1, keepdims=True))
        qb = qp.astype(BF16)
        sc = lax.dot_general(qb, kp.astype(BF16), _NT, preferred_element_type=F32) * jnp.exp(dmat - m_t)
        inter_w = jnp.exp(inter_log - m_t)
        num = (inter_w * jnp.dot(qb, c_prev.astype(BF16), preferred_element_type=F32)
               + jnp.dot(sc.astype(BF16), vh, preferred_element_type=F32))
        den = inter_w * jnp.sum(qp * n_prev, axis=-1, keepdims=True) + jnp.sum(sc, axis=-1, keepdims=True)
        hh = num / jnp.maximum(jnp.abs(den), jnp.exp(-m_t))

        g = b_col[L - 1:L, :]
        a_col = g - b_col + ig_col
        m_loc = jnp.max(a_col, axis=0, keepdims=True)
        kw = kp * jnp.exp(a_col - m_loc)
        kv = lax.dot_general(kw.astype(BF16), vh, _TN, preferred_element_type=F32)
        ks = jnp.sum(kw, axis=0, keepdims=True)
        m_new = jnp.maximum(g + m_prev, m_loc)
        decay = jnp.exp(g + m_prev - m_new)
        scale = jnp.exp(m_loc - m_new)
        c_ref[hd] = decay * c_prev + scale * kv
        n_ref[hd] = decay * n_prev + scale * ks
        m_ref[hd] = jnp.broadcast_to(m_new, (1, LANES))

        hn = _rms(hh, gn_ref[:, hd * M_DV:(hd + 1) * M_DV])
        out_ref[:, hd * M_DV:(hd + 1) * M_DV] = (_sigmoid(o_ref[:, hd * M_DV:(hd + 1) * M_DV]) * hn).astype(BF16)


def _mlstm(qk, mv, mo, gw, cw, cb, gb, gn):
    bsz, s, _ = qk.shape
    L = M_CHUNK
    blk = lambda width: pl.BlockSpec((None, L, width), lambda b, c: (b, c, 0))
    fixed = lambda shape: pl.BlockSpec(shape, lambda b, c: (0, 0))
    return pl.pallas_call(
        _mlstm_kernel,
        grid=(bsz, s // L),
        in_specs=[blk(QK_WIDTH), blk(M_WIDTH), blk(M_WIDTH), blk(LANES),
                  fixed((M_CONV, QK_WIDTH)), fixed((1, QK_WIDTH)), fixed((1, LANES)), fixed((1, M_WIDTH))],
        out_specs=blk(M_WIDTH),
        out_shape=jax.ShapeDtypeStruct((bsz, s, M_WIDTH), BF16),
        scratch_shapes=[pltpu.VMEM((L + SUBLANES, QK_WIDTH), F32),
                        pltpu.VMEM((M_HEADS, LANES, M_DV), F32),
                        pltpu.VMEM((M_HEADS, 1, LANES), F32),
                        pltpu.VMEM((M_HEADS, 1, LANES), F32)],
        compiler_params=pltpu.CompilerParams(dimension_semantics=("parallel", "arbitrary"),
                                             vmem_limit_bytes=VMEM_LIMIT),
        name="mlstm",
    )(qk, mv, mo, gw, cw, cb, gb, gn)


def _fold(x, op):
    return op(x.reshape(x.shape[0] // FOLD, FOLD, x.shape[1]), axis=0)


def _dsa_kernel(iq_ref, gw_ref, aq_ref, kn_ref, ak_ref, avt_ref, tb_ref, out_ref,
                score_ref, acc_ref, m_ref, l_ref, *, topk):
    i = pl.program_id(1)
    blocks_per_chunk = K_CHUNK // Q_TILE
    n_chunks = i // blocks_per_chunk + 1
    n_pairs = A_HEADS // 2
    t0 = i * Q_TILE
    kf = float(topk)

    def key_start(c):
        return pl.multiple_of((i - blocks_per_chunk * c) * Q_TILE, Q_TILE)

    def chunk_rows(c):
        return pl.ds(pl.multiple_of(c * K_CHUNK, K_CHUNK), K_CHUNK)

    lane_w = lax.broadcasted_iota(jnp.int32, (Q_TILE, A_WIDTH), 1)
    low_half = (lane_w % LANES) < A_DH

    def pair_rows(x):
        even = jnp.where(low_half, x, jnp.zeros_like(x))
        odd = jnp.where(low_half, jnp.zeros_like(x), x)
        return [jnp.concatenate([even[:, p * LANES:(p + 1) * LANES], odd[:, p * LANES:(p + 1) * LANES]], axis=0)
                for p in range(n_pairs)]

    key_off = lax.broadcasted_iota(jnp.int32, (K_CHUNK, Q_TILE), 0)
    q_off = lax.broadcasted_iota(jnp.int32, (K_CHUNK, Q_TILE), 1)

    def key_index(c):
        return key_start(c) - K_PAD + key_off

    iq_pairs = pair_rows(iq_ref[...])
    w_rows = gw_ref[...].T * (IDX_HEADS ** -0.5)

    def score_chunk(c, carry):
        mx, mn = carry
        kn = kn_ref[pl.ds(key_start(c), K_CHUNK), :]
        acc = jnp.zeros((K_CHUNK, Q_TILE), F32)
        for p in range(IDX_HEADS // 2):
            isc = lax.dot_general(kn, iq_pairs[p], _NT, preferred_element_type=F32)
            for e in range(2):
                r = 2 * M_HEADS + 2 * p + e
                acc = acc + jnp.maximum(isc[:, e * Q_TILE:(e + 1) * Q_TILE], 0.0) * w_rows[r:r + 1, :]
        key = key_index(c)
        ok = jnp.logical_and(key >= 0, key <= t0 + q_off)
        score_ref[chunk_rows(c), :] = jnp.where(ok, acc, -jnp.inf)
        mx = jnp.maximum(mx, _fold(jnp.where(ok, acc, -jnp.inf), jnp.max))
        mn = jnp.minimum(mn, _fold(jnp.where(ok, acc, jnp.inf), jnp.min))
        return mx, mn

    mx, mn = lax.fori_loop(0, n_chunks, score_chunk,
                           (jnp.full((FOLD, Q_TILE), -jnp.inf, F32), jnp.full((FOLD, Q_TILE), jnp.inf, F32)))
    q_max = jnp.max(mx, axis=0, keepdims=True)
    q_min = jnp.min(mn, axis=0, keepdims=True)

    def count(pred):
        def body(c, cnt):
            return cnt + _fold(jnp.where(pred(score_ref[chunk_rows(c), :], c), 1.0, 0.0), jnp.sum)
        return jnp.sum(lax.fori_loop(0, n_chunks, body, jnp.zeros((FOLD, Q_TILE), F32)), axis=0, keepdims=True)

    def count_ge(thr):
        return count(lambda x, c: x >= thr)

    def max_below(bound):
        def body(c, best):
            x = score_ref[chunk_rows(c), :]
            return jnp.maximum(best, _fold(jnp.where(x < bound, x, -jnp.inf), jnp.max))
        best = lax.fori_loop(0, n_chunks, body, jnp.full((FOLD, Q_TILE), -jnp.inf, F32))
        return jnp.max(best, axis=0, keepdims=True)

    def any_true(flag):
        return jnp.max(jnp.where(flag, 1, 0)) > 0

    t_row = t0 + lax.broadcasted_iota(jnp.int32, (1, Q_TILE), 1)
    active = (t_row + 1) > topk
    floor = jnp.full((1, Q_TILE), jnp.finfo(jnp.float32).min, F32)

    def search():
        def bisect(carry):
            lo, hi, hi_num, settled = carry
            mid = 0.5 * (lo + hi_num)
            cnt = count_ge(mid)
            open_ = settled == 0.0
            up = jnp.logical_and(open_, cnt >= kf)
            down = jnp.logical_and(open_, cnt < kf)
            settled = jnp.where(jnp.logical_and(open_, cnt == kf), 1.0, settled)
            return jnp.where(up, mid, lo), jnp.where(down, mid, hi), jnp.where(down, mid, hi_num), settled

        state = (q_min, jnp.full((1, Q_TILE), jnp.inf, F32), q_max, jnp.where(active, 0.0, 1.0))
        state = lax.fori_loop(0, BISECT_MIN, lambda _, st: bisect(st), state)

        def more(carry):
            step, pending = carry[0], carry[1]
            return jnp.logical_and(step < BISECT_MAX, pending)

        def again(carry):
            st = bisect(carry[2:])
            return (carry[0] + 1, any_true(st[3] == 0.0)) + st

        out = lax.while_loop(more, again, (jnp.int32(BISECT_MIN), any_true(state[3] == 0.0)) + state)
        lo, hi, _, settled = out[2:]
        pending = settled == 0.0

        def walk_down():
            def unfinished(cnt):
                return any_true(jnp.logical_and(pending, cnt < kf))

            def step(carry):
                hi_w, thr_w, cnt_w, _ = carry
                hi_w = jnp.where(cnt_w < kf, thr_w, hi_w)
                thr_w = max_below(hi_w)
                cnt_w = count_ge(thr_w)
                return hi_w, thr_w, cnt_w, unfinished(cnt_w)

            thr0 = max_below(hi)
            cnt0 = count_ge(thr0)
            _, thr_w, cnt_w, _ = lax.while_loop(lambda carry: carry[3], step, (hi, thr0, cnt0, unfinished(cnt0)))
            thr_w = jnp.where(pending, thr_w, lo)

            excess = jnp.logical_and(pending, cnt_w > kf)

            @pl.when(any_true(excess))
            def _():
                need = kf - count(lambda x, c: x > thr_w)

                def ties_below(pos):
                    return count(lambda x, c: jnp.logical_and(x == thr_w, key_index(c).astype(F32) < pos))

                def pos_search(_, carry):
                    lo_p, hi_p = carry
                    mid_p = jnp.floor(0.5 * (lo_p + hi_p))
                    ge = ties_below(mid_p) >= need
                    return jnp.where(ge, lo_p, mid_p + 1.0), jnp.where(ge, mid_p, hi_p)

                s_total = float(score_ref.shape[0])
                n_pos = int(math.ceil(math.log2(s_total + 1))) + 1
                _, pos = lax.fori_loop(0, n_pos, pos_search,
                                       (jnp.zeros((1, Q_TILE), F32), jnp.full((1, Q_TILE), s_total, F32)))
                pos = jnp.where(excess, pos, s_total)

                def demote(c, _):
                    x = score_ref[chunk_rows(c), :]
                    drop = jnp.logical_and(x == thr_w, key_index(c).astype(F32) >= pos)
                    score_ref[chunk_rows(c), :] = jnp.where(drop, -jnp.inf, x)
                    return 0

                lax.fori_loop(0, n_chunks, demote, 0)

            return thr_w

        thr = lax.cond(any_true(pending), walk_down, lambda: lo)
        return jnp.where(active, thr, floor)

    thr = lax.cond(t0 + Q_TILE > topk, search, lambda: floor)

    aq_pairs = pair_rows(aq_ref[...])
    near = K_CHUNK - 2 * Q_TILE
    m_ref[...] = jnp.full(m_ref.shape, NEG_BIG, F32)
    l_ref[...] = jnp.zeros(l_ref.shape, F32)
    acc_ref[...] = jnp.zeros(acc_ref.shape, F32)

    def attend(c, with_bias):
        start = key_start(c)
        madd = jnp.where(score_ref[chunk_rows(c), :] >= thr, 0.0, NEG_BIG)
        for p in range(n_pairs):
            cs = slice(p * LANES, (p + 1) * LANES)
            kp = ak_ref[pl.ds(start, K_CHUNK), cs]
            vt = avt_ref[cs, pl.ds(start, K_CHUNK)]
            s = lax.dot_general(kp, aq_pairs[p], _NT, preferred_element_type=F32)
            halves = []
            for e in range(2):
                se = s[:, e * Q_TILE:(e + 1) * Q_TILE] + madd
                if with_bias:
                    se = jnp.concatenate([se[:near], se[near:] + tb_ref[2 * p + e]], axis=0)
                halves.append(se)
            s = jnp.concatenate(halves, axis=1)
            m_old = m_ref[p]
            m_new = jnp.maximum(m_old, jnp.max(_fold(s, jnp.max), axis=0, keepdims=True))
            alpha = jnp.exp2(m_old - m_new)
            pe = jnp.exp2(s - m_new)
            l_ref[p] = alpha * l_ref[p] + jnp.sum(_fold(pe, jnp.sum), axis=0, keepdims=True)
            acc_ref[p] = alpha * acc_ref[p] + jnp.dot(vt, pe.astype(BF16), preferred_element_type=F32)
            m_ref[p] = m_new

    attend(0, True)
    lax.fori_loop(1, n_chunks, lambda c, _: (attend(c, False), 0)[1], 0)

    for p in range(n_pairs):
        acc = acc_ref[p]
        inv = 1.0 / l_ref[p]
        o = jnp.concatenate([acc[:A_DH, :Q_TILE] * inv[:, :Q_TILE], acc[A_DH:, Q_TILE:] * inv[:, Q_TILE:]], axis=0)
        out_ref[:, p * LANES:(p + 1) * LANES] = o.T.astype(BF16)


def _dsa(iq, gw, aq, kn, ak, avt, tb, topk):
    bsz, s, _ = iq.shape
    sp = s + K_PAD
    n_pairs = A_HEADS // 2
    qblk = lambda width: pl.BlockSpec((None, Q_TILE, width), lambda b, i: (b, i, 0))
    kblk = lambda width: pl.BlockSpec((None, sp, width), lambda b, i: (b, 0, 0))
    return pl.pallas_call(
        functools.partial(_dsa_kernel, topk=topk),
        grid=(bsz, s // Q_TILE),
        in_specs=[qblk(A_WIDTH), qblk(LANES), qblk(A_WIDTH), kblk(LANES), kblk(A_WIDTH),
                  pl.BlockSpec((None, A_WIDTH, sp), lambda b, i: (b, 0, 0)),
                  pl.BlockSpec(tb.shape, lambda b, i: (0, 0, 0))],
        out_specs=qblk(A_WIDTH),
        out_shape=jax.ShapeDtypeStruct((bsz, s, A_WIDTH), BF16),
        scratch_shapes=[pltpu.VMEM((s, Q_TILE), F32),
                        pltpu.VMEM((n_pairs, LANES, 2 * Q_TILE), F32),
                        pltpu.VMEM((n_pairs, 1, 2 * Q_TILE), F32),
                        pltpu.VMEM((n_pairs, 1, 2 * Q_TILE), F32)],
        compiler_params=pltpu.CompilerParams(dimension_semantics=("parallel", "arbitrary"),
                                             vmem_limit_bytes=VMEM_LIMIT),
        name="dsa",
    )(iq, gw, aq, kn, ak, avt, tb)


def _rel_bias_table(rel_bias):
    c = jnp.arange(2 * Q_TILE, dtype=jnp.int32)[:, None]
    a = jnp.arange(Q_TILE, dtype=jnp.int32)[None, :]
    dist = Q_TILE + a - c
    max_exact = REL_BUCKETS // 2
    d = jnp.maximum(dist, 0)
    large = max_exact + (jnp.log(jnp.maximum(d, 1).astype(F32) / max_exact)
                         / math.log(REL_MAX_DIST / max_exact) * (REL_BUCKETS - max_exact)).astype(jnp.int32)
    large = jnp.minimum(large, REL_BUCKETS - 1)
    bucket = jnp.where(d < max_exact, d, large)
    rb = rel_bias.astype(F32)
    onehot = (bucket[:, :, None] == jnp.arange(REL_BUCKETS, dtype=jnp.int32)).astype(F32)
    table = jnp.einsum("cak,kh->hca", onehot, rb - rb[REL_BUCKETS - 1], precision=lax.Precision.HIGHEST)
    return table * LOG2E


def _out_proj_kernel(x_ref, hm_ref, ha_ref, w_ref, g_ref, x1_ref, h2_ref):
    x1 = (x_ref[...] + jnp.dot(hm_ref[...], w_ref[0:M_WIDTH, :], preferred_element_type=F32)
          + jnp.dot(ha_ref[...], w_ref[M_WIDTH:, :], preferred_element_type=F32))
    x1_ref[...] = x1
    h2_ref[...] = _rms(x1, g_ref[...]).astype(BF16)


def _out_proj(x2, hm, ha, w, g):
    t, d = x2.shape
    row = lambda i: (i, 0)
    fixed = lambda i: (0, 0)
    return pl.pallas_call(
        _out_proj_kernel,
        grid=(t // ROW_TILE,),
        in_specs=[pl.BlockSpec((ROW_TILE, d), row), pl.BlockSpec((ROW_TILE, M_WIDTH), row),
                  pl.BlockSpec((ROW_TILE, A_WIDTH), row), pl.BlockSpec(w.shape, fixed), pl.BlockSpec((1, d), fixed)],
        out_specs=[pl.BlockSpec((ROW_TILE, d), row), pl.BlockSpec((ROW_TILE, d), row)],
        out_shape=[jax.ShapeDtypeStruct((t, d), F32), jax.ShapeDtypeStruct((t, d), BF16)],
        compiler_params=pltpu.CompilerParams(dimension_semantics=("parallel",), vmem_limit_bytes=VMEM_LIMIT),
        name="out_proj",
    )(x2, hm, ha, w, g)


def _ffn_kernel(x1_ref, h_ref, halo_ref, wg_ref, wv_ref, cwg_ref, cwv_ref, cbg_ref, cbv_ref, wd_ref, g_ref,
                out_ref, hext_ref, u_ref, a_ref, *, tiles_per_seq):
    tm = ROW_TILE
    first = (pl.program_id(0) % tiles_per_seq) == 0
    hext_ref[0:FFN_HALO, :] = jnp.where(first, jnp.zeros_like(halo_ref[...]), halo_ref[...])
    hext_ref[FFN_HALO:, :] = h_ref[...]
    hext = hext_ref[...]
    n_ff = wg_ref.shape[0]

    def conv(w_ref, cw_ref, cb_ref, c):
        u_ref[...] = jnp.dot(hext, w_ref[c], preferred_element_type=F32)
        y = cb_ref[c] + u_ref[FFN_HALO:, :] * cw_ref[c, FFN_CONV - 1:FFN_CONV, :]
        for j in range(FFN_CONV - 1):
            shift = FFN_CONV - 1 - j
            y = y + u_ref[FFN_HALO - shift:FFN_HALO - shift + tm, :] * cw_ref[c, j:j + 1, :]
        return y

    for c in range(n_ff):
        gate = conv(wg_ref, cwg_ref, cbg_ref, c)
        val = conv(wv_ref, cwv_ref, cbv_ref, c)
        a_ref[:, c * FF_CHUNK:(c + 1) * FF_CHUNK] = (gate * _sigmoid(gate) * val).astype(BF16)

    x2 = x1_ref[...] + jnp.dot(a_ref[...], wd_ref[...], preferred_element_type=F32)
    out_ref[...] = _rms(x2, g_ref[...])


def _ffn(x1, h2, wg, wv, cwg, cwv, cbg, cbv, wd, g, tiles_per_seq):
    t, d = x1.shape
    n_ff = wg.shape[0]
    row = lambda i: (i, 0)
    halo = lambda i: (jnp.maximum(i * (ROW_TILE // FFN_HALO) - 1, 0), 0)
    fixed2 = lambda i: (0, 0)
    fixed3 = lambda i: (0, 0, 0)
    return pl.pallas_call(
        functools.partial(_ffn_kernel, tiles_per_seq=tiles_per_seq),
        grid=(t // ROW_TILE,),
        in_specs=[pl.BlockSpec((ROW_TILE, d), row), pl.BlockSpec((ROW_TILE, d), row),
                  pl.BlockSpec((FFN_HALO, d), halo),
                  pl.BlockSpec(wg.shape, fixed3), pl.BlockSpec(wv.shape, fixed3),
                  pl.BlockSpec(cwg.shape, fixed3), pl.BlockSpec(cwv.shape, fixed3),
                  pl.BlockSpec(cbg.shape, fixed3), pl.BlockSpec(cbv.shape, fixed3),
                  pl.BlockSpec(wd.shape, fixed2), pl.BlockSpec((1, d), fixed2)],
        out_specs=pl.BlockSpec((ROW_TILE, d), row),
        out_shape=jax.ShapeDtypeStruct((t, d), F32),
        scratch_shapes=[pltpu.VMEM((ROW_TILE + FFN_HALO, d), BF16),
                        pltpu.VMEM((ROW_TILE + FFN_HALO, FF_CHUNK), F32),
                        pltpu.VMEM((ROW_TILE, n_ff * FF_CHUNK), BF16)],
        compiler_params=pltpu.CompilerParams(dimension_semantics=("parallel",), vmem_limit_bytes=VMEM_LIMIT),
        name="ffn",
    )(x1, h2, h2, wg, wv, cwg, cwv, cbg, cbv, wd, g)


def _forward(x, norm_mix, w_in, conv_w, conv_b, i_bias, f_bias, mlstm_norm, idx_k_norm, rel_bias,
             w_out, norm_ffn, w_up, ffn_conv_w, ffn_conv_b, w_down, norm_final):
    bsz, s, d = x.shape
    t = bsz * s
    assert t % ROW_TILE == 0 and s % ROW_TILE == 0 and s % M_CHUNK == 0 and s % K_CHUNK == 0
    topk = min(TOPK_MAX, s // 4)

    sizes = (M_HEADS * M_DQK, M_HEADS * M_DQK, M_WIDTH, M_WIDTH, M_HEADS, M_HEADS,
             A_WIDTH, A_WIDTH, A_WIDTH, IDX_HEADS * IDX_DIM, IDX_DIM, IDX_HEADS)
    offs = [0]
    for sz in sizes:
        offs.append(offs[-1] + sz)
    col = lambda k: w_in[:, offs[k]:offs[k + 1]]
    small_pad = jnp.zeros((d, LANES - 2 * M_HEADS - IDX_HEADS), w_in.dtype)
    w_cat = jnp.concatenate([col(0), col(1), col(2), col(3), col(6) * (A_DH ** -0.5 * LOG2E), col(7), col(8),
                             col(9) * (IDX_DIM ** -0.5), col(10), col(10), col(4), col(5), col(11), small_pad],
                            axis=1).astype(BF16)
    gk = jnp.concatenate([idx_k_norm, idx_k_norm]).reshape(1, LANES).astype(F32)

    x2 = x.reshape(t, d)
    qk, mv, mo, aq, ak, av, iq, kn, gw = _in_proj(x2, norm_mix.reshape(1, d).astype(F32), w_cat, gk)

    per_b = lambda a: a.reshape(bsz, s, a.shape[-1])
    gate_bias = jnp.concatenate([i_bias, f_bias, jnp.zeros((LANES - 2 * M_HEADS,), F32)]).reshape(1, LANES)
    hm = _mlstm(per_b(qk), per_b(mv), per_b(mo), per_b(gw), conv_w.astype(F32), conv_b.reshape(1, -1).astype(F32),
                gate_bias, mlstm_norm.reshape(1, -1).astype(F32))

    front = lambda a: jnp.pad(per_b(a), ((0, 0), (K_PAD, 0), (0, 0)))
    ha = _dsa(per_b(iq), per_b(gw), per_b(aq), front(kn), front(ak), jnp.swapaxes(front(av), 1, 2),
              _rel_bias_table(rel_bias), topk)

    x1, h2 = _out_proj(x2, hm.reshape(t, M_WIDTH), ha.reshape(t, A_WIDTH), w_out.astype(BF16),
                       norm_ffn.reshape(1, d).astype(F32))

    d_ff = w_down.shape[0]
    n_ff = d_ff // FF_CHUNK
    assert d_ff % FF_CHUNK == 0
    chunks = lambda w: jnp.transpose(w.reshape(w.shape[0], n_ff, FF_CHUNK), (1, 0, 2))
    wg = chunks(w_up[:, :d_ff]).astype(BF16)
    wv = chunks(w_up[:, d_ff:]).astype(BF16)
    cwg = chunks(ffn_conv_w[:, :d_ff]).astype(F32)
    cwv = chunks(ffn_conv_w[:, d_ff:]).astype(F32)
    cbg = ffn_conv_b[:d_ff].reshape(n_ff, 1, FF_CHUNK).astype(F32)
    cbv = ffn_conv_b[d_ff:].reshape(n_ff, 1, FF_CHUNK).astype(F32)
    out = _ffn(x1, h2, wg, wv, cwg, cwv, cbg, cbv, w_down.astype(BF16), norm_final.reshape(1, d).astype(F32),
               s // ROW_TILE)
    return hm, ha, x1.reshape(bsz, s, d), out.reshape(bsz, s, d)


def kernel(x, norm_mix, w_in, mlstm_conv_w, mlstm_conv_b, i_bias, f_bias, mlstm_norm, idx_k_norm, rel_bias,
           w_out, norm_ffn, w_up, ffn_conv_w, ffn_conv_b, w_down, norm_final):
    assert norm_mix.shape[0] == 1, "single-layer operation"
    return _forward(x, norm_mix[0], w_in[0], mlstm_conv_w[0], mlstm_conv_b[0], i_bias[0], f_bias[0],
                    mlstm_norm[0], idx_k_norm[0], rel_bias, w_out[0], norm_ffn[0], w_up[0], ffn_conv_w[0],
                    ffn_conv_b[0], w_down[0], norm_final)[3]
```

```python
import functools
import math

import jax
import jax.numpy as jnp
from jax import lax
from jax.experimental import pallas as pl
from jax.experimental.pallas import tpu as pltpu

F32 = jnp.float32
BF16 = jnp.bfloat16

EPS = 1e-6
LANES = 128
SUBLANES = 8

M_HEADS, M_DV, M_DQK, M_CONV = 4, 128, 64, 4
A_HEADS, A_DH = 8, 64
IDX_HEADS, IDX_DIM = 8, 64
TOPK_MAX = 256
REL_BUCKETS, REL_MAX_DIST = 32, 128
FFN_CONV = 3
M_WIDTH = M_HEADS * M_DV
A_WIDTH = A_HEADS * A_DH
QK_WIDTH = 2 * M_HEADS * M_DQK
LOG2E = math.log2(math.e)

ROW_TILE = 512
M_CHUNK = 256
Q_TILE = 128
K_CHUNK = 512
K_PAD = K_CHUNK - Q_TILE
FOLD = 4 * SUBLANES
FF_CHUNK = 256
FFN_HALO = 16
BISECT_MIN = 16
BISECT_MAX = 30
NEG_BIG = -1e30
VMEM_LIMIT = 56 * 1024 * 1024

_NT = (((1,), (1,)), ((), ()))
_TN = (((0,), (0,)), ((), ()))


def _rms(x, g):
    return x * lax.rsqrt(jnp.mean(x * x, axis=-1, keepdims=True) + EPS) * g


def _sigmoid(x):
    return 1.0 / (1.0 + jnp.exp(-x))


def _log_sigmoid(x):
    return jnp.minimum(x, 0.0) - jnp.log(1.0 + jnp.exp(-jnp.abs(x)))


def _in_proj_kernel(x_ref, g_ref, w_ref, gk_ref,
                    qk_ref, mv_ref, mo_ref, aq_ref, ak_ref, av_ref, iq_ref, kn_ref, gw_ref):
    h = _rms(x_ref[...], g_ref[...]).astype(BF16)

    def proj(col, width):
        return jnp.dot(h, w_ref[:, col:col + width], preferred_element_type=F32)

    qk_ref[...] = proj(0, 512)
    mv_ref[...] = proj(512, 512).astype(BF16)
    mo_ref[...] = proj(1024, 512)
    aq_ref[...] = proj(1536, 512).astype(BF16)
    ak_ref[...] = proj(2048, 512).astype(BF16)
    av_ref[...] = proj(2560, 512).astype(BF16)
    iq_ref[...] = proj(3072, 512).astype(BF16)
    kn_ref[...] = _rms(proj(3584, LANES), gk_ref[...]).astype(BF16)
    gw_ref[...] = proj(3712, LANES)


def _in_proj(x2, g, w, gk):
    t = x2.shape[0]
    d = x2.shape[1]
    row = lambda i: (i, 0)
    fixed = lambda i: (0, 0)
    wide = lambda dt: jax.ShapeDtypeStruct((t, 512), dt)
    return pl.pallas_call(
        _in_proj_kernel,
        grid=(t // ROW_TILE,),
        in_specs=[pl.BlockSpec((ROW_TILE, d), row), pl.BlockSpec((1, d), fixed),
                  pl.BlockSpec(w.shape, fixed), pl.BlockSpec((1, LANES), fixed)],
        out_specs=[pl.BlockSpec((ROW_TILE, 512), row)] * 7 + [pl.BlockSpec((ROW_TILE, LANES), row)] * 2,
        out_shape=[wide(F32), wide(BF16), wide(F32), wide(BF16), wide(BF16), wide(BF16), wide(BF16),
                   jax.ShapeDtypeStruct((t, LANES), BF16), jax.ShapeDtypeStruct((t, LANES), F32)],
        compiler_params=pltpu.CompilerParams(dimension_semantics=("parallel",), vmem_limit_bytes=VMEM_LIMIT),
        name="in_proj",
    )(x2, g, w, gk)


def _mlstm_kernel(qk_ref, v_ref, o_ref, gw_ref, cw_ref, cb_ref, gb_ref, gn_ref,
                  out_ref, xp_ref, c_ref, n_ref, m_ref):
    L = M_CHUNK

    @pl.when(pl.program_id(1) == 0)
    def _():
        xp_ref[0:SUBLANES, :] = jnp.zeros((SUBLANES, QK_WIDTH), F32)
        c_ref[...] = jnp.zeros_like(c_ref)
        n_ref[...] = jnp.zeros_like(n_ref)
        m_ref[...] = jnp.zeros_like(m_ref)

    x = qk_ref[...]
    xp_ref[SUBLANES:, :] = x
    y = cb_ref[...] + x * cw_ref[M_CONV - 1:M_CONV, :]
    for j in range(M_CONV - 1):
        shift = M_CONV - 1 - j
        y = y + xp_ref[SUBLANES - shift:SUBLANES - shift + L, :] * cw_ref[j:j + 1, :]
    xp_ref[0:SUBLANES, :] = x[L - SUBLANES:, :]
    y = y * _sigmoid(y)
    q_all = y[:, :M_HEADS * M_DQK] * (M_DQK ** -0.5)
    k_all = y[:, M_HEADS * M_DQK:]

    gw = gw_ref[...] + gb_ref[...]
    lane = lax.broadcasted_iota(jnp.int32, (L, LANES), 1)
    logf = _log_sigmoid(gw)
    r_i = lax.broadcasted_iota(jnp.int32, (L, L), 0)
    c_i = lax.broadcasted_iota(jnp.int32, (L, L), 1)
    tril = r_i >= c_i
    tril_b = tril.astype(BF16)
    hi = logf.astype(BF16)
    rem = logf - hi.astype(F32)
    mid = rem.astype(BF16)
    lo = (rem - mid.astype(F32)).astype(BF16)
    bcum = (jnp.dot(tril_b, hi, preferred_element_type=F32) + jnp.dot(tril_b, mid, preferred_element_type=F32)
            + jnp.dot(tril_b, lo, preferred_element_type=F32))
    cols = jnp.where(lane < M_HEADS, gw, bcum)
    rows = cols.T

    half = lax.broadcasted_iota(jnp.int32, (L, LANES), 1) < M_DQK
    for hd in range(M_HEADS):
        pair, odd = hd // 2, hd % 2
        sel = half if odd == 0 else jnp.logical_not(half)
        qp = jnp.where(sel, q_all[:, pair * LANES:(pair + 1) * LANES], 0.0)
        kp = k_all[:, pair * LANES:(pair + 1) * LANES]
        vh = v_ref[:, hd * M_DV:(hd + 1) * M_DV]
        ig_col = cols[:, hd:hd + 1]
        b_col = cols[:, M_HEADS + hd:M_HEADS + hd + 1]
        ig_row = rows[hd:hd + 1, :]
        b_row = rows[M_HEADS + hd:M_HEADS + hd + 1, :]
        c_prev = c_ref[hd]
        n_prev = n_ref[hd]
        m_prev = m_ref[hd][:, 0:1]

        dmat = jnp.where(tril, b_col - b_row + ig_row, -jnp.inf)
        inter_log = b_col + m_prev
        m_t = jnp.maximum(inter_log, jnp.max(dmat, axis=-1, keepdims=True))
        qb = qp.astype(BF16)
        sc = lax.dot_general(qb, kp.astype(BF16), _NT, preferred_element_type=F32) * jnp.exp(dmat - m_t)
        inter_w = jnp.exp(inter_log - m_t)
        num = (inter_w * jnp.dot(qb, c_prev.astype(BF16), preferred_element_type=F32)
               + jnp.dot(sc.astype(BF16), vh, preferred_element_type=F32))
        den = inter_w * jnp.sum(qp * n_prev, axis=-1, keepdims=True) + jnp.sum(sc, axis=-1, keepdims=True)
        hh = num / jnp.maximum(jnp.abs(den), jnp.exp(-m_t))

        g = b_col[L - 1:L, :]
        a_col = g - b_col + ig_col
        m_loc = jnp.max(a_col, axis=0, keepdims=True)
        kw = kp * jnp.exp(a_col - m_loc)
        kv = lax.dot_general(kw.astype(BF16), vh, _TN, preferred_element_type=F32)
        ks = jnp.sum(kw, axis=0, keepdims=True)
        m_new = jnp.maximum(g + m_prev, m_loc)
        decay = jnp.exp(g + m_prev - m_new)
        scale = jnp.exp(m_loc - m_new)
        c_ref[hd] = decay * c_prev + scale * kv
        n_ref[hd] = decay * n_prev + scale * ks
        m_ref[hd] = jnp.broadcast_to(m_new, (1, LANES))

        hn = _rms(hh, gn_ref[:, hd * M_DV:(hd + 1) * M_DV])
        out_ref[:, hd * M_DV:(hd + 1) * M_DV] = (_sigmoid(o_ref[:, hd * M_DV:(hd + 1) * M_DV]) * hn).astype(BF16)


def _mlstm(qk, mv, mo, gw, cw, cb, gb, gn):
    bsz, s, _ = qk.shape
    L = M_CHUNK
    blk = lambda width: pl.BlockSpec((None, L, width), lambda b, c: (b, c, 0))
    fixed = lambda shape: pl.BlockSpec(shape, lambda b, c: (0, 0))
    return pl.pallas_call(
        _mlstm_kernel,
        grid=(bsz, s // L),
        in_specs=[blk(QK_WIDTH), blk(M_WIDTH), blk(M_WIDTH), blk(LANES),
                  fixed((M_CONV, QK_WIDTH)), fixed((1, QK_WIDTH)), fixed((1, LANES)), fixed((1, M_WIDTH))],
        out_specs=blk(M_WIDTH),
        out_shape=jax.ShapeDtypeStruct((bsz, s, M_WIDTH), BF16),
        scratch_shapes=[pltpu.VMEM((L + SUBLANES, QK_WIDTH), F32),
                        pltpu.VMEM((M_HEADS, LANES, M_DV), F32),
                        pltpu.VMEM((M_HEADS, 1, LANES), F32),
                        pltpu.VMEM((M_HEADS, 1, LANES), F32)],
        compiler_params=pltpu.CompilerParams(dimension_semantics=("parallel", "arbitrary"),
                                             vmem_limit_bytes=VMEM_LIMIT),
        name="mlstm",
    )(qk, mv, mo, gw, cw, cb, gb, gn)


def _fold(x, op):
    return op(x.reshape(x.shape[0] // FOLD, FOLD, x.shape[1]), axis=0)


def _dsa_kernel(iq_ref, gw_ref, aq_ref, kn_ref, ak_ref, avt_ref, tb_ref, out_ref,
                score_ref, la_ref, lb_ref, acc_ref, m_ref, l_ref, *, topk):
    i = pl.program_id(1)
    blocks_per_chunk = K_CHUNK // Q_TILE
    n_chunks = i // blocks_per_chunk + 1
    n_pairs = A_HEADS // 2
    t0 = i * Q_TILE
    kf = float(topk)

    def key_start(c):
        return pl.multiple_of((i - blocks_per_chunk * c) * Q_TILE, Q_TILE)

    def chunk_rows(c):
        return pl.ds(pl.multiple_of(c * K_CHUNK, K_CHUNK), K_CHUNK)

    lane_w = lax.broadcasted_iota(jnp.int32, (Q_TILE, A_WIDTH), 1)
    low_half = (lane_w % LANES) < A_DH

    def pair_rows(x):
        even = jnp.where(low_half, x, jnp.zeros_like(x))
        odd = jnp.where(low_half, jnp.zeros_like(x), x)
        return [jnp.concatenate([even[:, p * LANES:(p + 1) * LANES], odd[:, p * LANES:(p + 1) * LANES]], axis=0)
                for p in range(n_pairs)]

    key_off = lax.broadcasted_iota(jnp.int32, (K_CHUNK, Q_TILE), 0)
    q_off = lax.broadcasted_iota(jnp.int32, (K_CHUNK, Q_TILE), 1)

    def key_index(c):
        return key_start(c) - K_PAD + key_off

    iq_pairs = pair_rows(iq_ref[...])
    w_rows = gw_ref[...].T * (IDX_HEADS ** -0.5)

    def score_chunk(c, carry):
        mx, mn = carry
        kn = kn_ref[pl.ds(key_start(c), K_CHUNK), :]
        acc = jnp.zeros((K_CHUNK, Q_TILE), F32)
        for p in range(IDX_HEADS // 2):
            isc = lax.dot_general(kn, iq_pairs[p], _NT, preferred_element_type=F32)
            for e in range(2):
                r = 2 * M_HEADS + 2 * p + e
                acc = acc + jnp.maximum(isc[:, e * Q_TILE:(e + 1) * Q_TILE], 0.0) * w_rows[r:r + 1, :]
        key = key_index(c)
        ok = jnp.logical_and(key >= 0, key <= t0 + q_off)
        score_ref[chunk_rows(c), :] = jnp.where(ok, acc, -jnp.inf)
        mx = jnp.maximum(mx, _fold(jnp.where(ok, acc, -jnp.inf), jnp.max))
        mn = jnp.minimum(mn, _fold(jnp.where(ok, acc, jnp.inf), jnp.min))
        return mx, mn

    mx, mn = lax.fori_loop(0, n_chunks, score_chunk,
                           (jnp.full((FOLD, Q_TILE), -jnp.inf, F32), jnp.full((FOLD, Q_TILE), jnp.inf, F32)))
    q_max = jnp.max(mx, axis=0, keepdims=True)
    q_min = jnp.min(mn, axis=0, keepdims=True)

    def count(pred):
        def body(c, cnt):
            return cnt + _fold(jnp.where(pred(score_ref[chunk_rows(c), :], c), 1.0, 0.0), jnp.sum)
        return jnp.sum(lax.fori_loop(0, n_chunks, body, jnp.zeros((FOLD, Q_TILE), F32)), axis=0, keepdims=True)

    def count_ge(thr):
        return count(lambda x, c: x >= thr)

    def max_below(bound):
        def body(c, best):
            x = score_ref[chunk_rows(c), :]
            return jnp.maximum(best, _fold(jnp.where(x < bound, x, -jnp.inf), jnp.max))
        best = lax.fori_loop(0, n_chunks, body, jnp.full((FOLD, Q_TILE), -jnp.inf, F32))
        return jnp.max(best, axis=0, keepdims=True)

    def any_true(flag):
        return jnp.max(jnp.where(flag, 1, 0)) > 0

    t_row = t0 + lax.broadcasted_iota(jnp.int32, (1, Q_TILE), 1)
    active = (t_row + 1) > topk
    floor = jnp.full((1, Q_TILE), jnp.finfo(jnp.float32).min, F32)

    def search():
        def bisect(carry):
            lo, hi, hi_num, settled = carry
            mid = 0.5 * (lo + hi_num)
            cnt = count_ge(mid)
            open_ = settled == 0.0
            up = jnp.logical_and(open_, cnt >= kf)
            down = jnp.logical_and(open_, cnt < kf)
            settled = jnp.where(jnp.logical_and(open_, cnt == kf), 1.0, settled)
            return jnp.where(up, mid, lo), jnp.where(down, mid, hi), jnp.where(down, mid, hi_num), settled

        state = (q_min, jnp.full((1, Q_TILE), jnp.inf, F32), q_max, jnp.where(active, 0.0, 1.0))
        state = lax.fori_loop(0, BISECT_MIN, lambda _, st: bisect(st), state)

        def more(carry):
            step, pending = carry[0], carry[1]
            return jnp.logical_and(step < BISECT_MAX, pending)

        def again(carry):
            st = bisect(carry[2:])
            return (carry[0] + 1, any_true(st[3] == 0.0)) + st

        out = lax.while_loop(more, again, (jnp.int32(BISECT_MIN), any_true(state[3] == 0.0)) + state)
        lo, hi, _, settled = out[2:]
        pending = settled == 0.0

        def walk_down():
            def unfinished(cnt):
                return any_true(jnp.logical_and(pending, cnt < kf))

            def step(carry):
                hi_w, thr_w, cnt_w, _ = carry
                hi_w = jnp.where(cnt_w < kf, thr_w, hi_w)
                thr_w = max_below(hi_w)
                cnt_w = count_ge(thr_w)
                return hi_w, thr_w, cnt_w, unfinished(cnt_w)

            thr0 = max_below(hi)
            cnt0 = count_ge(thr0)
            _, thr_w, cnt_w, _ = lax.while_loop(lambda carry: carry[3], step, (hi, thr0, cnt0, unfinished(cnt0)))
            thr_w = jnp.where(pending, thr_w, lo)

            excess = jnp.logical_and(pending, cnt_w > kf)

            @pl.when(any_true(excess))
            def _():
                need = kf - count(lambda x, c: x > thr_w)

                def ties_below(pos):
                    return count(lambda x, c: jnp.logical_and(x == thr_w, key_index(c).astype(F32) < pos))

                def pos_search(_, carry):
                    lo_p, hi_p = carry
                    mid_p = jnp.floor(0.5 * (lo_p + hi_p))
                    ge = ties_below(mid_p) >= need
                    return jnp.where(ge, lo_p, mid_p + 1.0), jnp.where(ge, mid_p, hi_p)

                s_total = float(score_ref.shape[0])
                n_pos = int(math.ceil(math.log2(s_total + 1))) + 1
                _, pos = lax.fori_loop(0, n_pos, pos_search,
                                       (jnp.zeros((1, Q_TILE), F32), jnp.full((1, Q_TILE), s_total, F32)))
                pos = jnp.where(excess, pos, s_total)

                def demote(c, _):
                    x = score_ref[chunk_rows(c), :]
                    drop = jnp.logical_and(x == thr_w, key_index(c).astype(F32) >= pos)
                    score_ref[chunk_rows(c), :] = jnp.where(drop, -jnp.inf, x)
                    return 0

                lax.fori_loop(0, n_chunks, demote, 0)

            return thr_w

        thr = lax.cond(any_true(pending), walk_down, lambda: lo)
        return jnp.where(active, thr, floor)

    thr = lax.cond(t0 + Q_TILE > topk, search, lambda: floor)

    aq_pairs = pair_rows(aq_ref[...])
    near = K_CHUNK - 2 * Q_TILE
    m_ref[...] = jnp.full(m_ref.shape, NEG_BIG, F32)
    l_ref[...] = jnp.zeros(l_ref.shape, F32)
    acc_ref[...] = jnp.zeros(acc_ref.shape, F32)

    def logits(c, with_bias, dst_ref):
        start = key_start(c)
        madd = jnp.where(score_ref[chunk_rows(c), :] >= thr, 0.0, NEG_BIG)
        for p in range(n_pairs):
            kp = ak_ref[pl.ds(start, K_CHUNK), p * LANES:(p + 1) * LANES]
            s = lax.dot_general(kp, aq_pairs[p], _NT, preferred_element_type=F32)
            for e in range(2):
                se = s[:, e * Q_TILE:(e + 1) * Q_TILE] + madd
                if with_bias:
                    se = jnp.concatenate([se[:near], se[near:] + tb_ref[2 * p + e]], axis=0)
                dst_ref[p, :, e * Q_TILE:(e + 1) * Q_TILE] = se

    def update(c, src_ref):
        start = key_start(c)
        for p in range(n_pairs):
            vt = avt_ref[p * LANES:(p + 1) * LANES, pl.ds(start, K_CHUNK)]
            s = src_ref[p]
            m_old = m_ref[p]
            m_new = jnp.maximum(m_old, jnp.max(_fold(s, jnp.max), axis=0, keepdims=True))
            alpha = jnp.exp2(m_old - m_new)
            pe = jnp.exp2(s - m_new)
            l_ref[p] = alpha * l_ref[p] + jnp.sum(_fold(pe, jnp.sum), axis=0, keepdims=True)
            acc_ref[p] = alpha * acc_ref[p] + jnp.dot(vt, pe.astype(BF16), preferred_element_type=F32)
            m_ref[p] = m_new

    def step(c, cur_ref, next_ref):
        logits(c + 1, False, next_ref)
        update(c, cur_ref)

    def pipelined(c, _):
        lax.cond(c % 2 == 0, lambda: step(c, la_ref, lb_ref), lambda: step(c, lb_ref, la_ref))
        return 0

    logits(0, True, la_ref)
    lax.fori_loop(0, n_chunks - 1, pipelined, 0)
    last = n_chunks - 1
    lax.cond(last % 2 == 0, lambda: update(last, la_ref), lambda: update(last, lb_ref))

    for p in range(n_pairs):
        acc = acc_ref[p]
        inv = 1.0 / l_ref[p]
        o = jnp.concatenate([acc[:A_DH, :Q_TILE] * inv[:, :Q_TILE], acc[A_DH:, Q_TILE:] * inv[:, Q_TILE:]], axis=0)
        out_ref[:, p * LANES:(p + 1) * LANES] = o.T.astype(BF16)


def _dsa(iq, gw, aq, kn, ak, avt, tb, topk):
    bsz, s, _ = iq.shape
    sp = s + K_PAD
    n_pairs = A_HEADS // 2
    qblk = lambda width: pl.BlockSpec((None, Q_TILE, width), lambda b, i: (b, i, 0))
    kblk = lambda width: pl.BlockSpec((None, sp, width), lambda b, i: (b, 0, 0))
    return pl.pallas_call(
        functools.partial(_dsa_kernel, topk=topk),
        grid=(bsz, s // Q_TILE),
        in_specs=[qblk(A_WIDTH), qblk(LANES), qblk(A_WIDTH), kblk(LANES), kblk(A_WIDTH),
                  pl.BlockSpec((None, A_WIDTH, sp), lambda b, i: (b, 0, 0)),
                  pl.BlockSpec(tb.shape, lambda b, i: (0, 0, 0))],
        out_specs=qblk(A_WIDTH),
        out_shape=jax.ShapeDtypeStruct((bsz, s, A_WIDTH), BF16),
        scratch_shapes=[pltpu.VMEM((s, Q_TILE), F32),
                        pltpu.VMEM((n_pairs, K_CHUNK, 2 * Q_TILE), F32),
                        pltpu.VMEM((n_pairs, K_CHUNK, 2 * Q_TILE), F32),
                        pltpu.VMEM((n_pairs, LANES, 2 * Q_TILE), F32),
                        pltpu.VMEM((n_pairs, 1, 2 * Q_TILE), F32),
                        pltpu.VMEM((n_pairs, 1, 2 * Q_TILE), F32)],
        compiler_params=pltpu.CompilerParams(dimension_semantics=("parallel", "arbitrary"),
                                             vmem_limit_bytes=VMEM_LIMIT),
        name="dsa",
    )(iq, gw, aq, kn, ak, avt, tb)


def _rel_bias_table(rel_bias):
    c = jnp.arange(2 * Q_TILE, dtype=jnp.int32)[:, None]
    a = jnp.arange(Q_TILE, dtype=jnp.int32)[None, :]
    dist = Q_TILE + a - c
    max_exact = REL_BUCKETS // 2
    d = jnp.maximum(dist, 0)
    large = max_exact + (jnp.log(jnp.maximum(d, 1).astype(F32) / max_exact)
                         / math.log(REL_MAX_DIST / max_exact) * (REL_BUCKETS - max_exact)).astype(jnp.int32)
    large = jnp.minimum(large, REL_BUCKETS - 1)
    bucket = jnp.where(d < max_exact, d, large)
    rb = rel_bias.astype(F32)
    onehot = (bucket[:, :, None] == jnp.arange(REL_BUCKETS, dtype=jnp.int32)).astype(F32)
    table = jnp.einsum("cak,kh->hca", onehot, rb - rb[REL_BUCKETS - 1], precision=lax.Precision.HIGHEST)
    return table * LOG2E


def _out_proj_kernel(x_ref, hm_ref, ha_ref, w_ref, g_ref, x1_ref, h2_ref):
    x1 = (x_ref[...] + jnp.dot(hm_ref[...], w_ref[0:M_WIDTH, :], preferred_element_type=F32)
          + jnp.dot(ha_ref[...], w_ref[M_WIDTH:, :], preferred_element_type=F32))
    x1_ref[...] = x1
    h2_ref[...] = _rms(x1, g_ref[...]).astype(BF16)


def _out_proj(x2, hm, ha, w, g):
    t, d = x2.shape
    row = lambda i: (i, 0)
    fixed = lambda i: (0, 0)
    return pl.pallas_call(
        _out_proj_kernel,
        grid=(t // ROW_TILE,),
        in_specs=[pl.BlockSpec((ROW_TILE, d), row), pl.BlockSpec((ROW_TILE, M_WIDTH), row),
                  pl.BlockSpec((ROW_TILE, A_WIDTH), row), pl.BlockSpec(w.shape, fixed), pl.BlockSpec((1, d), fixed)],
        out_specs=[pl.BlockSpec((ROW_TILE, d), row), pl.BlockSpec((ROW_TILE, d), row)],
        out_shape=[jax.ShapeDtypeStruct((t, d), F32), jax.ShapeDtypeStruct((t, d), BF16)],
        compiler_params=pltpu.CompilerParams(dimension_semantics=("parallel",), vmem_limit_bytes=VMEM_LIMIT),
        name="out_proj",
    )(x2, hm, ha, w, g)


def _ffn_kernel(x1_ref, h_ref, halo_ref, wg_ref, wv_ref, cwg_ref, cwv_ref, cbg_ref, cbv_ref, wd_ref, g_ref,
                out_ref, hext_ref, u_ref, a_ref, *, tiles_per_seq):
    tm = ROW_TILE
    first = (pl.program_id(0) % tiles_per_seq) == 0
    hext_ref[0:FFN_HALO, :] = jnp.where(first, jnp.zeros_like(halo_ref[...]), halo_ref[...])
    hext_ref[FFN_HALO:, :] = h_ref[...]
    hext = hext_ref[...]
    n_ff = wg_ref.shape[0]

    def conv(w_ref, cw_ref, cb_ref, c):
        u_ref[...] = jnp.dot(hext, w_ref[c], preferred_element_type=F32)
        y = cb_ref[c] + u_ref[FFN_HALO:, :] * cw_ref[c, FFN_CONV - 1:FFN_CONV, :]
        for j in range(FFN_CONV - 1):
            shift = FFN_CONV - 1 - j
            y = y + u_ref[FFN_HALO - shift:FFN_HALO - shift + tm, :] * cw_ref[c, j:j + 1, :]
        return y

    for c in range(n_ff):
        gate = conv(wg_ref, cwg_ref, cbg_ref, c)
        val = conv(wv_ref, cwv_ref, cbv_ref, c)
        a_ref[:, c * FF_CHUNK:(c + 1) * FF_CHUNK] = (gate * _sigmoid(gate) * val).astype(BF16)

    x2 = x1_ref[...] + jnp.dot(a_ref[...], wd_ref[...], preferred_element_type=F32)
    out_ref[...] = _rms(x2, g_ref[...])


def _ffn(x1, h2, wg, wv, cwg, cwv, cbg, cbv, wd, g, tiles_per_seq):
    t, d = x1.shape
    n_ff = wg.shape[0]
    row = lambda i: (i, 0)
    halo = lambda i: (jnp.maximum(i * (ROW_TILE // FFN_HALO) - 1, 0), 0)
    fixed2 = lambda i: (0, 0)
    fixed3 = lambda i: (0, 0, 0)
    return pl.pallas_call(
        functools.partial(_ffn_kernel, tiles_per_seq=tiles_per_seq),
        grid=(t // ROW_TILE,),
        in_specs=[pl.BlockSpec((ROW_TILE, d), row), pl.BlockSpec((ROW_TILE, d), row),
                  pl.BlockSpec((FFN_HALO, d), halo),
                  pl.BlockSpec(wg.shape, fixed3), pl.BlockSpec(wv.shape, fixed3),
                  pl.BlockSpec(cwg.shape, fixed3), pl.BlockSpec(cwv.shape, fixed3),
                  pl.BlockSpec(cbg.shape, fixed3), pl.BlockSpec(cbv.shape, fixed3),
                  pl.BlockSpec(wd.shape, fixed2), pl.BlockSpec((1, d), fixed2)],
        out_specs=pl.BlockSpec((ROW_TILE, d), row),
        out_shape=jax.ShapeDtypeStruct((t, d), F32),
        scratch_shapes=[pltpu.VMEM((ROW_TILE + FFN_HALO, d), BF16),
                        pltpu.VMEM((ROW_TILE + FFN_HALO, FF_CHUNK), F32),
                        pltpu.VMEM((ROW_TILE, n_ff * FF_CHUNK), BF16)],
        compiler_params=pltpu.CompilerParams(dimension_semantics=("parallel",), vmem_limit_bytes=VMEM_LIMIT),
        name="ffn",
    )(x1, h2, h2, wg, wv, cwg, cwv, cbg, cbv, wd, g)


def _forward(x, norm_mix, w_in, conv_w, conv_b, i_bias, f_bias, mlstm_norm, idx_k_norm, rel_bias,
             w_out, norm_ffn, w_up, ffn_conv_w, ffn_conv_b, w_down, norm_final):
    bsz, s, d = x.shape
    t = bsz * s
    assert t % ROW_TILE == 0 and s % ROW_TILE == 0 and s % M_CHUNK == 0 and s % K_CHUNK == 0
    topk = min(TOPK_MAX, s // 4)

    sizes = (M_HEADS * M_DQK, M_HEADS * M_DQK, M_WIDTH, M_WIDTH, M_HEADS, M_HEADS,
             A_WIDTH, A_WIDTH, A_WIDTH, IDX_HEADS * IDX_DIM, IDX_DIM, IDX_HEADS)
    offs = [0]
    for sz in sizes:
        offs.append(offs[-1] + sz)
    col = lambda k: w_in[:, offs[k]:offs[k + 1]]
    small_pad = jnp.zeros((d, LANES - 2 * M_HEADS - IDX_HEADS), w_in.dtype)
    w_cat = jnp.concatenate([col(0), col(1), col(2), col(3), col(6) * (A_DH ** -0.5 * LOG2E), col(7), col(8),
                             col(9) * (IDX_DIM ** -0.5), col(10), col(10), col(4), col(5), col(11), small_pad],
                            axis=1).astype(BF16)
    gk = jnp.concatenate([idx_k_norm, idx_k_norm]).reshape(1, LANES).astype(F32)

    x2 = x.reshape(t, d)
    qk, mv, mo, aq, ak, av, iq, kn, gw = _in_proj(x2, norm_mix.reshape(1, d).astype(F32), w_cat, gk)

    per_b = lambda a: a.reshape(bsz, s, a.shape[-1])
    gate_bias = jnp.concatenate([i_bias, f_bias, jnp.zeros((LANES - 2 * M_HEADS,), F32)]).reshape(1, LANES)
    hm = _mlstm(per_b(qk), per_b(mv), per_b(mo), per_b(gw), conv_w.astype(F32), conv_b.reshape(1, -1).astype(F32),
                gate_bias, mlstm_norm.reshape(1, -1).astype(F32))

    front = lambda a: jnp.pad(per_b(a), ((0, 0), (K_PAD, 0), (0, 0)))
    ha = _dsa(per_b(iq), per_b(gw), per_b(aq), front(kn), front(ak), jnp.swapaxes(front(av), 1, 2),
              _rel_bias_table(rel_bias), topk)

    x1, h2 = _out_proj(x2, hm.reshape(t, M_WIDTH), ha.reshape(t, A_WIDTH), w_out.astype(BF16),
                       norm_ffn.reshape(1, d).astype(F32))

    d_ff = w_down.shape[0]
    n_ff = d_ff // FF_CHUNK
    assert d_ff % FF_CHUNK == 0
    chunks = lambda w: jnp.transpose(w.reshape(w.shape[0], n_ff, FF_CHUNK), (1, 0, 2))
    wg = chunks(w_up[:, :d_ff]).astype(BF16)
    wv = chunks(w_up[:, d_ff:]).astype(BF16)
    cwg = chunks(ffn_conv_w[:, :d_ff]).astype(F32)
    cwv = chunks(ffn_conv_w[:, d_ff:]).astype(F32)
    cbg = ffn_conv_b[:d_ff].reshape(n_ff, 1, FF_CHUNK).astype(F32)
    cbv = ffn_conv_b[d_ff:].reshape(n_ff, 1, FF_CHUNK).astype(F32)
    out = _ffn(x1, h2, wg, wv, cwg, cwv, cbg, cbv, w_down.astype(BF16), norm_final.reshape(1, d).astype(F32),
               s // ROW_TILE)
    return hm, ha, x1.reshape(bsz, s, d), out.reshape(bsz, s, d)


def kernel(x, norm_mix, w_in, mlstm_conv_w, mlstm_conv_b, i_bias, f_bias, mlstm_norm, idx_k_norm, rel_bias,
           w_out, norm_ffn, w_up, ffn_conv_w, ffn_conv_b, w_down, norm_final):
    assert norm_mix.shape[0] == 1, "single-layer operation"
    return _forward(x, norm_mix[0], w_in[0], mlstm_conv_w[0], mlstm_conv_b[0], i_bias[0], f_bias[0],
                    mlstm_norm[0], idx_k_norm[0], rel_bias, w_out[0], norm_ffn[0], w_up[0], ffn_conv_w[0],
                    ffn_conv_b[0], w_down[0], norm_final)[3]
```

```python
import functools
import math

import jax
import jax.numpy as jnp
from jax import lax
from jax.experimental import pallas as pl
from jax.experimental.pallas import tpu as pltpu

F32 = jnp.float32
BF16 = jnp.bfloat16

EPS = 1e-6
LANES = 128
SUBLANES = 8

M_HEADS, M_DV, M_DQK, M_CONV = 4, 128, 64, 4
A_HEADS, A_DH = 8, 64
IDX_HEADS, IDX_DIM = 8, 64
TOPK_MAX = 256
REL_BUCKETS, REL_MAX_DIST = 32, 128
FFN_CONV = 3
M_WIDTH = M_HEADS * M_DV
A_WIDTH = A_HEADS * A_DH
QK_WIDTH = 2 * M_HEADS * M_DQK
LOG2E = math.log2(math.e)

ROW_TILE = 512
M_CHUNK = 256
Q_TILE = 128
K_CHUNK = 512
K_PAD = K_CHUNK - Q_TILE
FOLD = 4 * SUBLANES
SUM_ROWS = 2 * SUBLANES
FF_CHUNK = 256
FFN_HALO = 16
BISECT_MIN = 15
BISECT_GROUP = 3
BISECT_MAX = 30
BISECT_STALL = 2
NEG_BIG = -1e30
VMEM_LIMIT = 56 * 1024 * 1024

_NT = (((1,), (1,)), ((), ()))
_TN = (((0,), (0,)), ((), ()))


def _rms(x, g):
    return x * lax.rsqrt(jnp.mean(x * x, axis=-1, keepdims=True) + EPS) * g


def _sigmoid(x):
    return 1.0 / (1.0 + jnp.exp(-x))


def _log_sigmoid(x):
    return jnp.minimum(x, 0.0) - jnp.log(1.0 + jnp.exp(-jnp.abs(x)))


def _in_proj_kernel(x_ref, g_ref, w_ref, gk_ref,
                    qk_ref, mv_ref, mo_ref, aq_ref, ak_ref, av_ref, iq_ref, kn_ref, gw_ref):
    h = _rms(x_ref[...], g_ref[...]).astype(BF16)

    def proj(col, width):
        return jnp.dot(h, w_ref[:, col:col + width], preferred_element_type=F32)

    qk_ref[...] = proj(0, 512)
    mv_ref[...] = proj(512, 512).astype(BF16)
    mo_ref[...] = proj(1024, 512)
    aq_ref[...] = proj(1536, 512).astype(BF16)
    ak_ref[...] = proj(2048, 512).astype(BF16)
    av_ref[...] = proj(2560, 512).astype(BF16)
    iq_ref[...] = proj(3072, 512).astype(BF16)
    kn_ref[...] = _rms(proj(3584, LANES), gk_ref[...]).astype(BF16)
    gw_ref[...] = proj(3712, LANES)


def _in_proj(x2, g, w, gk):
    t = x2.shape[0]
    d = x2.shape[1]
    row = lambda i: (i, 0)
    fixed = lambda i: (0, 0)
    wide = lambda dt: jax.ShapeDtypeStruct((t, 512), dt)
    return pl.pallas_call(
        _in_proj_kernel,
        grid=(t // ROW_TILE,),
        in_specs=[pl.BlockSpec((ROW_TILE, d), row), pl.BlockSpec((1, d), fixed),
                  pl.BlockSpec(w.shape, fixed), pl.BlockSpec((1, LANES), fixed)],
        out_specs=[pl.BlockSpec((ROW_TILE, 512), row)] * 7 + [pl.BlockSpec((ROW_TILE, LANES), row)] * 2,
        out_shape=[wide(F32), wide(BF16), wide(F32), wide(BF16), wide(BF16), wide(BF16), wide(BF16),
                   jax.ShapeDtypeStruct((t, LANES), BF16), jax.ShapeDtypeStruct((t, LANES), F32)],
        compiler_params=pltpu.CompilerParams(dimension_semantics=("parallel",), vmem_limit_bytes=VMEM_LIMIT),
        name="in_proj",
    )(x2, g, w, gk)


def _mlstm_kernel(qk_ref, v_ref, o_ref, gw_ref, cw_ref, cb_ref, gb_ref, gn_ref,
                  out_ref, xp_ref, c_ref, n_ref, m_ref):
    L = M_CHUNK

    @pl.when(pl.program_id(1) == 0)
    def _():
        xp_ref[0:SUBLANES, :] = jnp.zeros((SUBLANES, QK_WIDTH), F32)
        c_ref[...] = jnp.zeros_like(c_ref)
        n_ref[...] = jnp.zeros_like(n_ref)
        m_ref[...] = jnp.zeros_like(m_ref)

    x = qk_ref[...]
    xp_ref[SUBLANES:, :] = x
    y = cb_ref[...] + x * cw_ref[M_CONV - 1:M_CONV, :]
    for j in range(M_CONV - 1):
        shift = M_CONV - 1 - j
        y = y + xp_ref[SUBLANES - shift:SUBLANES - shift + L, :] * cw_ref[j:j + 1, :]
    xp_ref[0:SUBLANES, :] = x[L - SUBLANES:, :]
    y = y * _sigmoid(y)
    q_all = y[:, :M_HEADS * M_DQK] * (M_DQK ** -0.5)
    k_all = y[:, M_HEADS * M_DQK:]

    gw = gw_ref[...] + gb_ref[...]
    lane = lax.broadcasted_iota(jnp.int32, (L, LANES), 1)
    logf = _log_sigmoid(gw)
    r_i = lax.broadcasted_iota(jnp.int32, (L, L), 0)
    c_i = lax.broadcasted_iota(jnp.int32, (L, L), 1)
    tril = r_i >= c_i
    tril_b = tril.astype(BF16)
    hi = logf.astype(BF16)
    rem = logf - hi.astype(F32)
    mid = rem.astype(BF16)
    lo = (rem - mid.astype(F32)).astype(BF16)
    bcum = (jnp.dot(tril_b, hi, preferred_element_type=F32) + jnp.dot(tril_b, mid, preferred_element_type=F32)
            + jnp.dot(tril_b, lo, preferred_element_type=F32))
    cols = jnp.where(lane < M_HEADS, gw, bcum)
    rows = cols.T

    half = lax.broadcasted_iota(jnp.int32, (L, LANES), 1) < M_DQK
    for hd in range(M_HEADS):
        pair, odd = hd // 2, hd % 2
        sel = half if odd == 0 else jnp.logical_not(half)
        qp = jnp.where(sel, q_all[:, pair * LANES:(pair + 1) * LANES], 0.0)
        kp = k_all[:, pair * LANES:(pair + 1) * LANES]
        vh = v_ref[:, hd * M_DV:(hd + 1) * M_DV]
        ig_col = cols[:, hd:hd + 1]
        b_col = cols[:, M_HEADS + hd:M_HEADS + hd + 1]
        ig_row = rows[hd:hd + 1, :]
        b_row = rows[M_HEADS + hd:M_HEADS + hd + 1, :]
        c_prev = c_ref[hd]
        n_prev = n_ref[hd]
        m_prev = m_ref[hd][:, 0:1]

        dmat = jnp.where(tril, b_col - b_row + ig_row, -jnp.inf)
        inter_log = b_col + m_prev
        m_t = jnp.maximum(inter_log, jnp.max(dmat, axis=-1, keepdims=True))
        qb = qp.astype(BF16)
        sc = lax.dot_general(qb, kp.astype(BF16), _NT, preferred_element_type=F32) * jnp.exp(dmat - m_t)
        inter_w = jnp.exp(inter_log - m_t)
        num = (inter_w * jnp.dot(qb, c_prev.astype(BF16), preferred_element_type=F32)
               + jnp.dot(sc.astype(BF16), vh, preferred_element_type=F32))
        den = inter_w * jnp.sum(qp * n_prev, axis=-1, keepdims=True) + jnp.sum(sc, axis=-1, keepdims=True)
        hh = num / jnp.maximum(jnp.abs(den), jnp.exp(-m_t))

        g = b_col[L - 1:L, :]
        a_col = g - b_col + ig_col
        m_loc = jnp.max(a_col, axis=0, keepdims=True)
        kw = kp * jnp.exp(a_col - m_loc)
        kv = lax.dot_general(kw.astype(BF16), vh, _TN, preferred_element_type=F32)
        ks = jnp.sum(kw, axis=0, keepdims=True)
        m_new = jnp.maximum(g + m_prev, m_loc)
        decay = jnp.exp(g + m_prev - m_new)
        scale = jnp.exp(m_loc - m_new)
        c_ref[hd] = decay * c_prev + scale * kv
        n_ref[hd] = decay * n_prev + scale * ks
        m_ref[hd] = jnp.broadcast_to(m_new, (1, LANES))

        hn = _rms(hh, gn_ref[:, hd * M_DV:(hd + 1) * M_DV])
        out_ref[:, hd * M_DV:(hd + 1) * M_DV] = (_sigmoid(o_ref[:, hd * M_DV:(hd + 1) * M_DV]) * hn).astype(BF16)


def _mlstm(qk, mv, mo, gw, cw, cb, gb, gn):
    bsz, s, _ = qk.shape
    L = M_CHUNK
    blk = lambda width: pl.BlockSpec((None, L, width), lambda b, c: (b, c, 0))
    fixed = lambda shape: pl.BlockSpec(shape, lambda b, c: (0, 0))
    return pl.pallas_call(
        _mlstm_kernel,
        grid=(bsz, s // L),
        in_specs=[blk(QK_WIDTH), blk(M_WIDTH), blk(M_WIDTH), blk(LANES),
                  fixed((M_CONV, QK_WIDTH)), fixed((1, QK_WIDTH)), fixed((1, LANES)), fixed((1, M_WIDTH))],
        out_specs=blk(M_WIDTH),
        out_shape=jax.ShapeDtypeStruct((bsz, s, M_WIDTH), BF16),
        scratch_shapes=[pltpu.VMEM((L + SUBLANES, QK_WIDTH), F32),
                        pltpu.VMEM((M_HEADS, LANES, M_DV), F32),
                        pltpu.VMEM((M_HEADS, 1, LANES), F32),
                        pltpu.VMEM((M_HEADS, 1, LANES), F32)],
        compiler_params=pltpu.CompilerParams(dimension_semantics=("parallel", "arbitrary"),
                                             vmem_limit_bytes=VMEM_LIMIT),
        name="mlstm",
    )(qk, mv, mo, gw, cw, cb, gb, gn)


def _fold(x, op):
    return op(x.reshape(x.shape[0] // FOLD, FOLD, x.shape[1]), axis=0)


def _dsa_kernel(iq_ref, gw_ref, aq_ref, kn_ref, ak_ref, avt_ref, tb_ref, out_ref,
                score_ref, la_ref, lb_ref, acc_ref, m_ref, *, topk):
    i = pl.program_id(1)
    blocks_per_chunk = K_CHUNK // Q_TILE
    n_chunks = i // blocks_per_chunk + 1
    n_pairs = A_HEADS // 2
    t0 = i * Q_TILE
    kf = float(topk)

    def key_start(c):
        return pl.multiple_of((i - blocks_per_chunk * c) * Q_TILE, Q_TILE)

    def chunk_rows(c):
        return pl.ds(pl.multiple_of(c * K_CHUNK, K_CHUNK), K_CHUNK)

    lane_w = lax.broadcasted_iota(jnp.int32, (Q_TILE, A_WIDTH), 1)
    low_half = (lane_w % LANES) < A_DH

    def pair_rows(x):
        even = jnp.where(low_half, x, jnp.zeros_like(x))
        odd = jnp.where(low_half, jnp.zeros_like(x), x)
        return [jnp.concatenate([even[:, p * LANES:(p + 1) * LANES], odd[:, p * LANES:(p + 1) * LANES]], axis=0)
                for p in range(n_pairs)]

    key_off = lax.broadcasted_iota(jnp.int32, (K_CHUNK, Q_TILE), 0)
    q_off = lax.broadcasted_iota(jnp.int32, (K_CHUNK, Q_TILE), 1)

    def key_index(c):
        return key_start(c) - K_PAD + key_off

    iq_pairs = pair_rows(iq_ref[...])
    w_rows = gw_ref[...].T * (IDX_HEADS ** -0.5)

    def score_chunk(c, carry):
        mx, mn = carry
        kn = kn_ref[pl.ds(key_start(c), K_CHUNK), :]
        acc = jnp.zeros((K_CHUNK, Q_TILE), F32)
        for p in range(IDX_HEADS // 2):
            isc = lax.dot_general(kn, iq_pairs[p], _NT, preferred_element_type=F32)
            for e in range(2):
                r = 2 * M_HEADS + 2 * p + e
                acc = acc + jnp.maximum(isc[:, e * Q_TILE:(e + 1) * Q_TILE], 0.0) * w_rows[r:r + 1, :]
        key = key_index(c)
        ok = jnp.logical_and(key >= 0, key <= t0 + q_off)
        score_ref[chunk_rows(c), :] = jnp.where(ok, acc, -jnp.inf)
        mx = jnp.maximum(mx, _fold(jnp.where(ok, acc, -jnp.inf), jnp.max))
        mn = jnp.minimum(mn, _fold(jnp.where(ok, acc, jnp.inf), jnp.min))
        return mx, mn

    mx, mn = lax.fori_loop(0, n_chunks, score_chunk,
                           (jnp.full((FOLD, Q_TILE), -jnp.inf, F32), jnp.full((FOLD, Q_TILE), jnp.inf, F32)))
    q_max = jnp.max(mx, axis=0, keepdims=True)
    q_min = jnp.min(mn, axis=0, keepdims=True)

    def count(pred):
        def body(c, cnt):
            return cnt + _fold(jnp.where(pred(score_ref[chunk_rows(c), :], c), 1.0, 0.0), jnp.sum)
        return jnp.sum(lax.fori_loop(0, n_chunks, body, jnp.zeros((FOLD, Q_TILE), F32)), axis=0, keepdims=True)

    def count_ge(thr):
        return count(lambda x, c: x >= thr)

    def max_below(bound):
        def body(c, best):
            x = score_ref[chunk_rows(c), :]
            return jnp.maximum(best, _fold(jnp.where(x < bound, x, -jnp.inf), jnp.max))
        best = lax.fori_loop(0, n_chunks, body, jnp.full((FOLD, Q_TILE), -jnp.inf, F32))
        return jnp.max(best, axis=0, keepdims=True)

    def any_true(flag):
        return jnp.max(jnp.where(flag, 1, 0)) > 0

    t_row = t0 + lax.broadcasted_iota(jnp.int32, (1, Q_TILE), 1)
    active = (t_row + 1) > topk
    floor = jnp.full((1, Q_TILE), jnp.finfo(jnp.float32).min, F32)

    def search():
        def bisect(carry):
            lo, hi, hi_num, settled, c_lo, c_hi = carry
            mid = 0.5 * (lo + hi_num)
            cnt = count_ge(mid)
            open_ = settled == 0.0
            up = jnp.logical_and(open_, cnt >= kf)
            down = jnp.logical_and(open_, cnt < kf)
            settled = jnp.where(jnp.logical_and(open_, cnt == kf), 1.0, settled)
            return (jnp.where(up, mid, lo), jnp.where(down, mid, hi), jnp.where(down, mid, hi_num), settled,
                    jnp.where(up, cnt, c_lo), jnp.where(down, cnt, c_hi))

        state = (q_min, jnp.full((1, Q_TILE), jnp.inf, F32), q_max, jnp.where(active, 0.0, 1.0),
                 jnp.full((1, Q_TILE), float(score_ref.shape[0]), F32), jnp.zeros((1, Q_TILE), F32))
        state = lax.fori_loop(0, BISECT_MIN, lambda _, st: bisect(st), state)

        def more(carry):
            step, pending, moving = carry[0], carry[1], carry[2]
            return jnp.logical_and(jnp.logical_and(step < BISECT_MAX, pending), moving)

        def again(carry):
            st = carry[3:]
            moved = jnp.zeros((1, Q_TILE), F32)
            for g in range(BISECT_GROUP):
                new = bisect(st)
                if g >= BISECT_GROUP - BISECT_STALL:
                    sep = jnp.logical_and(st[3] == 0.0, jnp.logical_or(new[4] != st[4], new[5] != st[5]))
                    moved = jnp.where(sep, 1.0, moved)
                st = new
            code = jnp.sum(jnp.where(st[3] == 0.0, 1.0, 0.0) + (2.0 * Q_TILE) * moved)
            n_moved = jnp.floor(code / (2.0 * Q_TILE))
            return (carry[0] + BISECT_GROUP, code - (2.0 * Q_TILE) * n_moved > 0.5, n_moved > 0.5) + st

        out = lax.while_loop(more, again,
                             (jnp.int32(BISECT_MIN), any_true(state[3] == 0.0), jnp.bool_(True)) + state)
        lo, hi, _, settled = out[3:7]
        pending = settled == 0.0

        def walk_down():
            def unfinished(cnt):
                return any_true(jnp.logical_and(pending, cnt < kf))

            def step(carry):
                hi_w, thr_w, cnt_w, _ = carry
                hi_w = jnp.where(cnt_w < kf, thr_w, hi_w)
                thr_w = max_below(hi_w)
                cnt_w = count_ge(thr_w)
                return hi_w, thr_w, cnt_w, unfinished(cnt_w)

            thr0 = max_below(hi)
            cnt0 = count_ge(thr0)
            _, thr_w, cnt_w, _ = lax.while_loop(lambda carry: carry[3], step, (hi, thr0, cnt0, unfinished(cnt0)))
            thr_w = jnp.where(pending, thr_w, lo)

            excess = jnp.logical_and(pending, cnt_w > kf)

            @pl.when(any_true(excess))
            def _():
                need = jnp.where(excess, kf - count(lambda x, c: x > thr_w), float(score_ref.shape[0]))
                r_i = lax.broadcasted_iota(jnp.int32, (K_CHUNK, K_CHUNK), 0)
                c_i = lax.broadcasted_iota(jnp.int32, (K_CHUNK, K_CHUNK), 1)
                earlier = (c_i < r_i).astype(BF16)

                def demote(j, before):
                    c = n_chunks - 1 - j
                    x = score_ref[chunk_rows(c), :]
                    tied = jnp.where(x == thr_w, 1.0, 0.0)
                    rank = before + jnp.dot(earlier, tied.astype(BF16), preferred_element_type=F32)
                    drop = jnp.logical_and(x == thr_w, rank >= need)
                    score_ref[chunk_rows(c), :] = jnp.where(drop, -jnp.inf, x)
                    return before + jnp.sum(_fold(tied, jnp.sum), axis=0, keepdims=True)

                lax.fori_loop(0, n_chunks, demote, jnp.zeros((1, Q_TILE), F32))

            return thr_w

        thr = lax.cond(any_true(pending), walk_down, lambda: lo)
        return jnp.where(active, thr, floor)

    thr = lax.cond(t0 + Q_TILE > topk, search, lambda: floor)

    aq_pairs = pair_rows(aq_ref[...])
    near = K_CHUNK - 2 * Q_TILE
    m_ref[...] = jnp.full(m_ref.shape, NEG_BIG, F32)
    acc_ref[...] = jnp.zeros(acc_ref.shape, F32)
    ones_rows = jnp.ones((SUM_ROWS, K_CHUNK), BF16)

    def logits(c, with_bias, dst_ref):
        start = key_start(c)
        madd = jnp.where(score_ref[chunk_rows(c), :] >= thr, 0.0, NEG_BIG)
        for p in range(n_pairs):
            kp = ak_ref[pl.ds(start, K_CHUNK), p * LANES:(p + 1) * LANES]
            s = lax.dot_general(kp, aq_pairs[p], _NT, preferred_element_type=F32)
            for e in range(2):
                se = s[:, e * Q_TILE:(e + 1) * Q_TILE] + madd
                if with_bias:
                    se = jnp.concatenate([se[:near], se[near:] + tb_ref[2 * p + e]], axis=0)
                dst_ref[p, :, e * Q_TILE:(e + 1) * Q_TILE] = se

    def update(c, src_ref):
        start = key_start(c)
        for p in range(n_pairs):
            vt = jnp.concatenate([avt_ref[p * LANES:(p + 1) * LANES, pl.ds(start, K_CHUNK)], ones_rows], axis=0)
            s = src_ref[p]
            m_old = m_ref[p]
            m_new = jnp.maximum(m_old, jnp.max(_fold(s, jnp.max), axis=0, keepdims=True))
            alpha = jnp.exp2(m_old - m_new)
            pe = jnp.exp2(s - m_new)
            acc_ref[p] = alpha * acc_ref[p] + jnp.dot(vt, pe.astype(BF16), preferred_element_type=F32)
            m_ref[p] = m_new

    def step(c, cur_ref, next_ref):
        logits(c + 1, False, next_ref)
        update(c, cur_ref)

    def pipelined(c, _):
        lax.cond(c % 2 == 0, lambda: step(c, la_ref, lb_ref), lambda: step(c, lb_ref, la_ref))
        return 0

    logits(0, True, la_ref)
    lax.fori_loop(0, n_chunks - 1, pipelined, 0)
    last = n_chunks - 1
    lax.cond(last % 2 == 0, lambda: update(last, la_ref), lambda: update(last, lb_ref))

    for p in range(n_pairs):
        acc = acc_ref[p]
        inv = 1.0 / acc[LANES:LANES + 1, :]
        o = jnp.concatenate([acc[:A_DH, :Q_TILE] * inv[:, :Q_TILE], acc[A_DH:LANES, Q_TILE:] * inv[:, Q_TILE:]],
                            axis=0)
        out_ref[:, p * LANES:(p + 1) * LANES] = o.T.astype(BF16)


def _dsa(iq, gw, aq, kn, ak, avt, tb, topk):
    bsz, s, _ = iq.shape
    sp = s + K_PAD
    n_pairs = A_HEADS // 2
    qblk = lambda width: pl.BlockSpec((None, Q_TILE, width), lambda b, i: (b, i, 0))
    kblk = lambda width: pl.BlockSpec((None, sp, width), lambda b, i: (b, 0, 0))
    return pl.pallas_call(
        functools.partial(_dsa_kernel, topk=topk),
        grid=(bsz, s // Q_TILE),
        in_specs=[qblk(A_WIDTH), qblk(LANES), qblk(A_WIDTH), kblk(LANES), kblk(A_WIDTH),
                  pl.BlockSpec((None, A_WIDTH, sp), lambda b, i: (b, 0, 0)),
                  pl.BlockSpec(tb.shape, lambda b, i: (0, 0, 0))],
        out_specs=qblk(A_WIDTH),
        out_shape=jax.ShapeDtypeStruct((bsz, s, A_WIDTH), BF16),
        scratch_shapes=[pltpu.VMEM((s, Q_TILE), F32),
                        pltpu.VMEM((n_pairs, K_CHUNK, 2 * Q_TILE), F32),
                        pltpu.VMEM((n_pairs, K_CHUNK, 2 * Q_TILE), F32),
                        pltpu.VMEM((n_pairs, LANES + SUM_ROWS, 2 * Q_TILE), F32),
                        pltpu.VMEM((n_pairs, 1, 2 * Q_TILE), F32)],
        compiler_params=pltpu.CompilerParams(dimension_semantics=("parallel", "arbitrary"),
                                             vmem_limit_bytes=VMEM_LIMIT),
        name="dsa",
    )(iq, gw, aq, kn, ak, avt, tb)


def _rel_bias_table(rel_bias):
    c = jnp.arange(2 * Q_TILE, dtype=jnp.int32)[:, None]
    a = jnp.arange(Q_TILE, dtype=jnp.int32)[None, :]
    dist = Q_TILE + a - c
    max_exact = REL_BUCKETS // 2
    d = jnp.maximum(dist, 0)
    large = max_exact + (jnp.log(jnp.maximum(d, 1).astype(F32) / max_exact)
                         / math.log(REL_MAX_DIST / max_exact) * (REL_BUCKETS - max_exact)).astype(jnp.int32)
    large = jnp.minimum(large, REL_BUCKETS - 1)
    bucket = jnp.where(d < max_exact, d, large)
    rb = rel_bias.astype(F32)
    onehot = (bucket[:, :, None] == jnp.arange(REL_BUCKETS, dtype=jnp.int32)).astype(F32)
    table = jnp.einsum("cak,kh->hca", onehot, rb - rb[REL_BUCKETS - 1], precision=lax.Precision.HIGHEST)
    return table * LOG2E


def _out_proj_kernel(x_ref, hm_ref, ha_ref, w_ref, g_ref, x1_ref, h2_ref):
    x1 = (x_ref[...] + jnp.dot(hm_ref[...], w_ref[0:M_WIDTH, :], preferred_element_type=F32)
          + jnp.dot(ha_ref[...], w_ref[M_WIDTH:, :], preferred_element_type=F32))
    x1_ref[...] = x1
    h2_ref[...] = _rms(x1, g_ref[...]).astype(BF16)


def _out_proj(x2, hm, ha, w, g):
    t, d = x2.shape
    row = lambda i: (i, 0)
    fixed = lambda i: (0, 0)
    return pl.pallas_call(
        _out_proj_kernel,
        grid=(t // ROW_TILE,),
        in_specs=[pl.BlockSpec((ROW_TILE, d), row), pl.BlockSpec((ROW_TILE, M_WIDTH), row),
                  pl.BlockSpec((ROW_TILE, A_WIDTH), row), pl.BlockSpec(w.shape, fixed), pl.BlockSpec((1, d), fixed)],
        out_specs=[pl.BlockSpec((ROW_TILE, d), row), pl.BlockSpec((ROW_TILE, d), row)],
        out_shape=[jax.ShapeDtypeStruct((t, d), F32), jax.ShapeDtypeStruct((t, d), BF16)],
        compiler_params=pltpu.CompilerParams(dimension_semantics=("parallel",), vmem_limit_bytes=VMEM_LIMIT),
        name="out_proj",
    )(x2, hm, ha, w, g)


def _ffn_kernel(x1_ref, h_ref, halo_ref, wg_ref, wv_ref, cwg_ref, cwv_ref, cbg_ref, cbv_ref, wd_ref, g_ref,
                out_ref, hext_ref, u_ref, a_ref, *, tiles_per_seq):
    tm = ROW_TILE
    first = (pl.program_id(0) % tiles_per_seq) == 0
    hext_ref[0:FFN_HALO, :] = jnp.where(first, jnp.zeros_like(halo_ref[...]), halo_ref[...])
    hext_ref[FFN_HALO:, :] = h_ref[...]
    hext = hext_ref[...]
    n_ff = wg_ref.shape[0]

    def conv(w_ref, cw_ref, cb_ref, c):
        u_ref[...] = jnp.dot(hext, w_ref[c], preferred_element_type=F32)
        y = cb_ref[c] + u_ref[FFN_HALO:, :] * cw_ref[c, FFN_CONV - 1:FFN_CONV, :]
        for j in range(FFN_CONV - 1):
            shift = FFN_CONV - 1 - j
            y = y + u_ref[FFN_HALO - shift:FFN_HALO - shift + tm, :] * cw_ref[c, j:j + 1, :]
        return y

    for c in range(n_ff):
        gate = conv(wg_ref, cwg_ref, cbg_ref, c)
        val = conv(wv_ref, cwv_ref, cbv_ref, c)
        a_ref[:, c * FF_CHUNK:(c + 1) * FF_CHUNK] = (gate * _sigmoid(gate) * val).astype(BF16)

    x2 = x1_ref[...] + jnp.dot(a_ref[...], wd_ref[...], preferred_element_type=F32)
    out_ref[...] = _rms(x2, g_ref[...])


def _ffn(x1, h2, wg, wv, cwg, cwv, cbg, cbv, wd, g, tiles_per_seq):
    t, d = x1.shape
    n_ff = wg.shape[0]
    row = lambda i: (i, 0)
    halo = lambda i: (jnp.maximum(i * (ROW_TILE // FFN_HALO) - 1, 0), 0)
    fixed2 = lambda i: (0, 0)
    fixed3 = lambda i: (0, 0, 0)
    return pl.pallas_call(
        functools.partial(_ffn_kernel, tiles_per_seq=tiles_per_seq),
        grid=(t // ROW_TILE,),
        in_specs=[pl.BlockSpec((ROW_TILE, d), row), pl.BlockSpec((ROW_TILE, d), row),
                  pl.BlockSpec((FFN_HALO, d), halo),
                  pl.BlockSpec(wg.shape, fixed3), pl.BlockSpec(wv.shape, fixed3),
                  pl.BlockSpec(cwg.shape, fixed3), pl.BlockSpec(cwv.shape, fixed3),
                  pl.BlockSpec(cbg.shape, fixed3), pl.BlockSpec(cbv.shape, fixed3),
                  pl.BlockSpec(wd.shape, fixed2), pl.BlockSpec((1, d), fixed2)],
        out_specs=pl.BlockSpec((ROW_TILE, d), row),
        out_shape=jax.ShapeDtypeStruct((t, d), F32),
        scratch_shapes=[pltpu.VMEM((ROW_TILE + FFN_HALO, d), BF16),
                        pltpu.VMEM((ROW_TILE + FFN_HALO, FF_CHUNK), F32),
                        pltpu.VMEM((ROW_TILE, n_ff * FF_CHUNK), BF16)],
        compiler_params=pltpu.CompilerParams(dimension_semantics=("parallel",), vmem_limit_bytes=VMEM_LIMIT),
        name="ffn",
    )(x1, h2, h2, wg, wv, cwg, cwv, cbg, cbv, wd, g)


def _forward(x, norm_mix, w_in, conv_w, conv_b, i_bias, f_bias, mlstm_norm, idx_k_norm, rel_bias,
             w_out, norm_ffn, w_up, ffn_conv_w, ffn_conv_b, w_down, norm_final):
    bsz, s, d = x.shape
    t = bsz * s
    assert t % ROW_TILE == 0 and s % ROW_TILE == 0 and s % M_CHUNK == 0 and s % K_CHUNK == 0
    topk = min(TOPK_MAX, s // 4)

    sizes = (M_HEADS * M_DQK, M_HEADS * M_DQK, M_WIDTH, M_WIDTH, M_HEADS, M_HEADS,
             A_WIDTH, A_WIDTH, A_WIDTH, IDX_HEADS * IDX_DIM, IDX_DIM, IDX_HEADS)
    offs = [0]
    for sz in sizes:
        offs.append(offs[-1] + sz)
    col = lambda k: w_in[:, offs[k]:offs[k + 1]]
    small_pad = jnp.zeros((d, LANES - 2 * M_HEADS - IDX_HEADS), w_in.dtype)
    w_cat = jnp.concatenate([col(0), col(1), col(2), col(3), col(6) * (A_DH ** -0.5 * LOG2E), col(7), col(8),
                             col(9) * (IDX_DIM ** -0.5), col(10), col(10), col(4), col(5), col(11), small_pad],
                            axis=1).astype(BF16)
    gk = jnp.concatenate([idx_k_norm, idx_k_norm]).reshape(1, LANES).astype(F32)

    x2 = x.reshape(t, d)
    qk, mv, mo, aq, ak, av, iq, kn, gw = _in_proj(x2, norm_mix.reshape(1, d).astype(F32), w_cat, gk)

    per_b = lambda a: a.reshape(bsz, s, a.shape[-1])
    gate_bias = jnp.concatenate([i_bias, f_bias, jnp.zeros((LANES - 2 * M_HEADS,), F32)]).reshape(1, LANES)
    hm = _mlstm(per_b(qk), per_b(mv), per_b(mo), per_b(gw), conv_w.astype(F32), conv_b.reshape(1, -1).astype(F32),
                gate_bias, mlstm_norm.reshape(1, -1).astype(F32))

    front = lambda a: jnp.pad(per_b(a), ((0, 0), (K_PAD, 0), (0, 0)))
    ha = _dsa(per_b(iq), per_b(gw), per_b(aq), front(kn), front(ak), jnp.swapaxes(front(av), 1, 2),
              _rel_bias_table(rel_bias), topk)

    x1, h2 = _out_proj(x2, hm.reshape(t, M_WIDTH), ha.reshape(t, A_WIDTH), w_out.astype(BF16),
                       norm_ffn.reshape(1, d).astype(F32))

    d_ff = w_down.shape[0]
    n_ff = d_ff // FF_CHUNK
    assert d_ff % FF_CHUNK == 0
    chunks = lambda w: jnp.transpose(w.reshape(w.shape[0], n_ff, FF_CHUNK), (1, 0, 2))
    wg = chunks(w_up[:, :d_ff]).astype(BF16)
    wv = chunks(w_up[:, d_ff:]).astype(BF16)
    cwg = chunks(ffn_conv_w[:, :d_ff]).astype(F32)
    cwv = chunks(ffn_conv_w[:, d_ff:]).astype(F32)
    cbg = ffn_conv_b[:d_ff].reshape(n_ff, 1, FF_CHUNK).astype(F32)
    cbv = ffn_conv_b[d_ff:].reshape(n_ff, 1, FF_CHUNK).astype(F32)
    out = _ffn(x1, h2, wg, wv, cwg, cwv, cbg, cbv, w_down.astype(BF16), norm_final.reshape(1, d).astype(F32),
               s // ROW_TILE)
    return hm, ha, x1.reshape(bsz, s, d), out.reshape(bsz, s, d)


def kernel(x, norm_mix, w_in, mlstm_conv_w, mlstm_conv_b, i_bias, f_bias, mlstm_norm, idx_k_norm, rel_bias,
           w_out, norm_ffn, w_up, ffn_conv_w, ffn_conv_b, w_down, norm_final):
    assert norm_mix.shape[0] == 1, "single-layer operation"
    return _forward(x, norm_mix[0], w_in[0], mlstm_conv_w[0], mlstm_conv_b[0], i_bias[0], f_bias[0],
                    mlstm_norm[0], idx_k_norm[0], rel_bias, w_out[0], norm_ffn[0], w_up[0], ffn_conv_w[0],
                    ffn_conv_b[0], w_down[0], norm_final)[3]
```

```python
import functools
import math

import jax
import jax.numpy as jnp
from jax import lax
from jax.experimental import pallas as pl
from jax.experimental.pallas import tpu as pltpu

F32 = jnp.float32
BF16 = jnp.bfloat16

EPS = 1e-6
LANES = 128
SUBLANES = 8

M_HEADS, M_DV, M_DQK, M_CONV = 4, 128, 64, 4
A_HEADS, A_DH = 8, 64
IDX_HEADS, IDX_DIM = 8, 64
TOPK_MAX = 256
REL_BUCKETS, REL_MAX_DIST = 32, 128
FFN_CONV = 3
M_WIDTH = M_HEADS * M_DV
A_WIDTH = A_HEADS * A_DH
QK_WIDTH = 2 * M_HEADS * M_DQK
LOG2E = math.log2(math.e)

ROW_TILE = 512
M_CHUNK = 256
Q_TILE = 128
K_CHUNK = 512
K_PAD = ROW_TILE
FOLD = 4 * SUBLANES
SUM_ROWS = 2 * SUBLANES
FF_CHUNK = 256
FFN_HALO = 16
BISECT_MIN = 15
BISECT_GROUP = 3
BISECT_MAX = 30
BISECT_STALL = 2
NEG_BIG = -1e30
VMEM_LIMIT = 56 * 1024 * 1024

_NT = (((1,), (1,)), ((), ()))
_TN = (((0,), (0,)), ((), ()))


def _rms(x, g):
    return x * lax.rsqrt(jnp.mean(x * x, axis=-1, keepdims=True) + EPS) * g


def _sigmoid(x):
    return 1.0 / (1.0 + jnp.exp(-x))


def _log_sigmoid(x):
    return jnp.minimum(x, 0.0) - jnp.log(1.0 + jnp.exp(-jnp.abs(x)))


def _in_proj_kernel(x_ref, g_ref, w_ref, gk_ref,
                    qk_ref, mv_ref, mo_ref, aq_ref, iq_ref, gw_ref, ak_ref, avt_ref, kn_ref):
    @pl.when(pl.program_id(1) == 0)
    def _():
        ak_ref[...] = jnp.zeros_like(ak_ref)
        avt_ref[...] = jnp.zeros_like(avt_ref)
        kn_ref[...] = jnp.zeros_like(kn_ref)

    @pl.when(pl.program_id(1) > 0)
    def _():
        h = _rms(x_ref[...], g_ref[...]).astype(BF16)

        def proj(col, width):
            return jnp.dot(h, w_ref[:, col:col + width], preferred_element_type=F32)

        qk_ref[...] = proj(0, 512)
        mv_ref[...] = proj(512, 512).astype(BF16)
        mo_ref[...] = proj(1024, 512)
        aq_ref[...] = proj(1536, 512).astype(BF16)
        ak_ref[...] = proj(2048, 512).astype(BF16)
        avt_ref[...] = proj(2560, 512).T.astype(BF16)
        iq_ref[...] = proj(3072, 512).astype(BF16)
        kn_ref[...] = _rms(proj(3584, LANES), gk_ref[...]).astype(BF16)
        gw_ref[...] = proj(3712, LANES)


def _in_proj(x, g, w, gk):
    bsz, s, d = x.shape
    assert K_PAD == ROW_TILE
    row = lambda b, j: (b, jnp.maximum(j - 1, 0), 0)
    padded = lambda b, j: (b, j, 0)
    fixed = lambda b, j: (0, 0)
    blk = lambda width, index: pl.BlockSpec((None, ROW_TILE, width), index)
    wide = lambda dt: jax.ShapeDtypeStruct((bsz, s, 512), dt)
    return pl.pallas_call(
        _in_proj_kernel,
        grid=(bsz, s // ROW_TILE + 1),
        in_specs=[blk(d, row), pl.BlockSpec((1, d), fixed), pl.BlockSpec(w.shape, fixed),
                  pl.BlockSpec((1, LANES), fixed)],
        out_specs=[blk(512, row)] * 5 + [blk(LANES, row), blk(512, padded),
                                         pl.BlockSpec((None, 512, ROW_TILE), lambda b, j: (b, 0, j)),
                                         blk(LANES, padded)],
        out_shape=[wide(F32), wide(BF16), wide(F32), wide(BF16), wide(BF16),
                   jax.ShapeDtypeStruct((bsz, s, LANES), F32),
                   jax.ShapeDtypeStruct((bsz, s + K_PAD, 512), BF16),
                   jax.ShapeDtypeStruct((bsz, 512, s + K_PAD), BF16),
                   jax.ShapeDtypeStruct((bsz, s + K_PAD, LANES), BF16)],
        compiler_params=pltpu.CompilerParams(dimension_semantics=("parallel", "arbitrary"),
                                             vmem_limit_bytes=VMEM_LIMIT),
        name="in_proj",
    )(x, g, w, gk)


def _mlstm_kernel(qk_ref, v_ref, o_ref, gw_ref, cw_ref, cb_ref, gb_ref, gn_ref,
                  out_ref, xp_ref, c_ref, n_ref, m_ref):
    L = M_CHUNK

    @pl.when(pl.program_id(1) == 0)
    def _():
        xp_ref[0:SUBLANES, :] = jnp.zeros((SUBLANES, QK_WIDTH), F32)
        c_ref[...] = jnp.zeros_like(c_ref)
        n_ref[...] = jnp.zeros_like(n_ref)
        m_ref[...] = jnp.zeros_like(m_ref)

    x = qk_ref[...]
    xp_ref[SUBLANES:, :] = x
    y = cb_ref[...] + x * cw_ref[M_CONV - 1:M_CONV, :]
    for j in range(M_CONV - 1):
        shift = M_CONV - 1 - j
        y = y + xp_ref[SUBLANES - shift:SUBLANES - shift + L, :] * cw_ref[j:j + 1, :]
    xp_ref[0:SUBLANES, :] = x[L - SUBLANES:, :]
    y = y * _sigmoid(y)
    q_all = y[:, :M_HEADS * M_DQK] * (M_DQK ** -0.5)
    k_all = y[:, M_HEADS * M_DQK:]

    gw = gw_ref[...] + gb_ref[...]
    lane = lax.broadcasted_iota(jnp.int32, (L, LANES), 1)
    logf = _log_sigmoid(gw)
    r_i = lax.broadcasted_iota(jnp.int32, (L, L), 0)
    c_i = lax.broadcasted_iota(jnp.int32, (L, L), 1)
    tril = r_i >= c_i
    tril_b = tril.astype(BF16)
    hi = logf.astype(BF16)
    rem = logf - hi.astype(F32)
    mid = rem.astype(BF16)
    lo = (rem - mid.astype(F32)).astype(BF16)
    bcum = (jnp.dot(tril_b, hi, preferred_element_type=F32) + jnp.dot(tril_b, mid, preferred_element_type=F32)
            + jnp.dot(tril_b, lo, preferred_element_type=F32))
    cols = jnp.where(lane < M_HEADS, gw, bcum)
    rows = cols.T

    half = lax.broadcasted_iota(jnp.int32, (L, LANES), 1) < M_DQK
    for hd in range(M_HEADS):
        pair, odd = hd // 2, hd % 2
        sel = half if odd == 0 else jnp.logical_not(half)
        qp = jnp.where(sel, q_all[:, pair * LANES:(pair + 1) * LANES], 0.0)
        kp = k_all[:, pair * LANES:(pair + 1) * LANES]
        vh = v_ref[:, hd * M_DV:(hd + 1) * M_DV]
        ig_col = cols[:, hd:hd + 1]
        b_col = cols[:, M_HEADS + hd:M_HEADS + hd + 1]
        ig_row = rows[hd:hd + 1, :]
        b_row = rows[M_HEADS + hd:M_HEADS + hd + 1, :]
        c_prev = c_ref[hd]
        n_prev = n_ref[hd]
        m_prev = m_ref[hd][:, 0:1]

        dmat = jnp.where(tril, b_col - b_row + ig_row, -jnp.inf)
        inter_log = b_col + m_prev
        m_t = jnp.maximum(inter_log, jnp.max(dmat, axis=-1, keepdims=True))
        qb = qp.astype(BF16)
        sc = lax.dot_general(qb, kp.astype(BF16), _NT, preferred_element_type=F32) * jnp.exp(dmat - m_t)
        inter_w = jnp.exp(inter_log - m_t)
        num = (inter_w * jnp.dot(qb, c_prev.astype(BF16), preferred_element_type=F32)
               + jnp.dot(sc.astype(BF16), vh, preferred_element_type=F32))
        den = inter_w * jnp.sum(qp * n_prev, axis=-1, keepdims=True) + jnp.sum(sc, axis=-1, keepdims=True)
        hh = num / jnp.maximum(jnp.abs(den), jnp.exp(-m_t))

        g = b_col[L - 1:L, :]
        a_col = g - b_col + ig_col
        m_loc = jnp.max(a_col, axis=0, keepdims=True)
        kw = kp * jnp.exp(a_col - m_loc)
        kv = lax.dot_general(kw.astype(BF16), vh, _TN, preferred_element_type=F32)
        ks = jnp.sum(kw, axis=0, keepdims=True)
        m_new = jnp.maximum(g + m_prev, m_loc)
        decay = jnp.exp(g + m_prev - m_new)
        scale = jnp.exp(m_loc - m_new)
        c_ref[hd] = decay * c_prev + scale * kv
        n_ref[hd] = decay * n_prev + scale * ks
        m_ref[hd] = jnp.broadcast_to(m_new, (1, LANES))

        hn = _rms(hh, gn_ref[:, hd * M_DV:(hd + 1) * M_DV])
        out_ref[:, hd * M_DV:(hd + 1) * M_DV] = (_sigmoid(o_ref[:, hd * M_DV:(hd + 1) * M_DV]) * hn).astype(BF16)


def _mlstm(qk, mv, mo, gw, cw, cb, gb, gn):
    bsz, s, _ = qk.shape
    L = M_CHUNK
    blk = lambda width: pl.BlockSpec((None, L, width), lambda b, c: (b, c, 0))
    fixed = lambda shape: pl.BlockSpec(shape, lambda b, c: (0, 0))
    return pl.pallas_call(
        _mlstm_kernel,
        grid=(bsz, s // L),
        in_specs=[blk(QK_WIDTH), blk(M_WIDTH), blk(M_WIDTH), blk(LANES),
                  fixed((M_CONV, QK_WIDTH)), fixed((1, QK_WIDTH)), fixed((1, LANES)), fixed((1, M_WIDTH))],
        out_specs=blk(M_WIDTH),
        out_shape=jax.ShapeDtypeStruct((bsz, s, M_WIDTH), BF16),
        scratch_shapes=[pltpu.VMEM((L + SUBLANES, QK_WIDTH), F32),
                        pltpu.VMEM((M_HEADS, LANES, M_DV), F32),
                        pltpu.VMEM((M_HEADS, 1, LANES), F32),
                        pltpu.VMEM((M_HEADS, 1, LANES), F32)],
        compiler_params=pltpu.CompilerParams(dimension_semantics=("parallel", "arbitrary"),
                                             vmem_limit_bytes=VMEM_LIMIT),
        name="mlstm",
    )(qk, mv, mo, gw, cw, cb, gb, gn)


def _fold(x, op):
    return op(x.reshape(x.shape[0] // FOLD, FOLD, x.shape[1]), axis=0)


def _dsa_kernel(iq_ref, gw_ref, aq_ref, kn_ref, ak_ref, avt_ref, tb_ref, out_ref,
                score_ref, la_ref, lb_ref, acc_ref, m_ref, *, topk):
    i = pl.program_id(1)
    blocks_per_chunk = K_CHUNK // Q_TILE
    n_chunks = i // blocks_per_chunk + 1
    n_pairs = A_HEADS // 2
    t0 = i * Q_TILE
    kf = float(topk)

    def key_start(c):
        return pl.multiple_of((i - blocks_per_chunk * c) * Q_TILE + (K_PAD - K_CHUNK + Q_TILE), Q_TILE)

    def chunk_rows(c):
        return pl.ds(pl.multiple_of(c * K_CHUNK, K_CHUNK), K_CHUNK)

    lane_w = lax.broadcasted_iota(jnp.int32, (Q_TILE, A_WIDTH), 1)
    low_half = (lane_w % LANES) < A_DH

    def pair_rows(x):
        even = jnp.where(low_half, x, jnp.zeros_like(x))
        odd = jnp.where(low_half, jnp.zeros_like(x), x)
        return [jnp.concatenate([even[:, p * LANES:(p + 1) * LANES], odd[:, p * LANES:(p + 1) * LANES]], axis=0)
                for p in range(n_pairs)]

    key_off = lax.broadcasted_iota(jnp.int32, (K_CHUNK, Q_TILE), 0)
    q_off = lax.broadcasted_iota(jnp.int32, (K_CHUNK, Q_TILE), 1)

    def key_index(c):
        return key_start(c) - K_PAD + key_off

    iq_all = jnp.concatenate(pair_rows(iq_ref[...]), axis=0)
    w_rows = gw_ref[...].T * (IDX_HEADS ** -0.5)

    def score_chunk(c, carry):
        mx, mn = carry
        kn = kn_ref[pl.ds(key_start(c), K_CHUNK), :]
        acc = jnp.zeros((K_CHUNK, Q_TILE), F32)
        isc = lax.dot_general(kn, iq_all, _NT, preferred_element_type=F32)
        for hd in range(IDX_HEADS):
            r = 2 * M_HEADS + hd
            acc = acc + jnp.maximum(isc[:, hd * Q_TILE:(hd + 1) * Q_TILE], 0.0) * w_rows[r:r + 1, :]
        key = key_index(c)
        ok = jnp.logical_and(key >= 0, key <= t0 + q_off)
        score_ref[chunk_rows(c), :] = jnp.where(ok, acc, -jnp.inf)
        mx = jnp.maximum(mx, _fold(jnp.where(ok, acc, -jnp.inf), jnp.max))
        mn = jnp.minimum(mn, _fold(jnp.where(ok, acc, jnp.inf), jnp.min))
        return mx, mn

    extremes = (jnp.full((FOLD, Q_TILE), -jnp.inf, F32), jnp.full((FOLD, Q_TILE), jnp.inf, F32))
    extremes = lax.fori_loop(0, n_chunks // 2, lambda j, ex: score_chunk(2 * j + 1, score_chunk(2 * j, ex)), extremes)
    mx, mn = lax.cond(n_chunks % 2 == 1, lambda ex: score_chunk(n_chunks - 1, ex), lambda ex: ex, extremes)
    q_max = jnp.max(mx, axis=0, keepdims=True)
    q_min = jnp.min(mn, axis=0, keepdims=True)

    def count(pred):
        def body(c, cnt):
            return cnt + _fold(jnp.where(pred(score_ref[chunk_rows(c), :], c), 1.0, 0.0), jnp.sum)
        return jnp.sum(lax.fori_loop(0, n_chunks, body, jnp.zeros((FOLD, Q_TILE), F32)), axis=0, keepdims=True)

    def count_ge(thr):
        return count(lambda x, c: x >= thr)

    def max_below(bound):
        def body(c, best):
            x = score_ref[chunk_rows(c), :]
            return jnp.maximum(best, _fold(jnp.where(x < bound, x, -jnp.inf), jnp.max))
        best = lax.fori_loop(0, n_chunks, body, jnp.full((FOLD, Q_TILE), -jnp.inf, F32))
        return jnp.max(best, axis=0, keepdims=True)

    def any_true(flag):
        return jnp.max(jnp.where(flag, 1, 0)) > 0

    t_row = t0 + lax.broadcasted_iota(jnp.int32, (1, Q_TILE), 1)
    active = (t_row + 1) > topk
    floor = jnp.full((1, Q_TILE), jnp.finfo(jnp.float32).min, F32)

    def search():
        def bisect(carry):
            lo, hi, hi_num, settled, c_lo, c_hi = carry
            mid = 0.5 * (lo + hi_num)
            cnt = count_ge(mid)
            open_ = settled == 0.0
            up = jnp.logical_and(open_, cnt >= kf)
            down = jnp.logical_and(open_, cnt < kf)
            settled = jnp.where(jnp.logical_and(open_, cnt == kf), 1.0, settled)
            return (jnp.where(up, mid, lo), jnp.where(down, mid, hi), jnp.where(down, mid, hi_num), settled,
                    jnp.where(up, cnt, c_lo), jnp.where(down, cnt, c_hi))

        state = (q_min, jnp.full((1, Q_TILE), jnp.inf, F32), q_max, jnp.where(active, 0.0, 1.0),
                 jnp.full((1, Q_TILE), float(score_ref.shape[0]), F32), jnp.zeros((1, Q_TILE), F32))
        state = lax.fori_loop(0, BISECT_MIN, lambda _, st: bisect(st), state)

        def more(carry):
            step, pending, moving = carry[0], carry[1], carry[2]
            return jnp.logical_and(jnp.logical_and(step < BISECT_MAX, pending), moving)

        def again(carry):
            st = carry[3:]
            moved = jnp.zeros((1, Q_TILE), F32)
            for g in range(BISECT_GROUP):
                new = bisect(st)
                if g >= BISECT_GROUP - BISECT_STALL:
                    sep = jnp.logical_and(st[3] == 0.0, jnp.logical_or(new[4] != st[4], new[5] != st[5]))
                    moved = jnp.where(sep, 1.0, moved)
                st = new
            code = jnp.sum(jnp.where(st[3] == 0.0, 1.0, 0.0) + (2.0 * Q_TILE) * moved)
            n_moved = jnp.floor(code / (2.0 * Q_TILE))
            return (carry[0] + BISECT_GROUP, code - (2.0 * Q_TILE) * n_moved > 0.5, n_moved > 0.5) + st

        out = lax.while_loop(more, again,
                             (jnp.int32(BISECT_MIN), any_true(state[3] == 0.0), jnp.bool_(True)) + state)
        lo, hi, _, settled = out[3:7]
        pending = settled == 0.0

        def walk_down():
            def unfinished(cnt):
                return any_true(jnp.logical_and(pending, cnt < kf))

            def step(carry):
                hi_w, thr_w, cnt_w, _ = carry
                hi_w = jnp.where(cnt_w < kf, thr_w, hi_w)
                thr_w = max_below(hi_w)
                cnt_w = count_ge(thr_w)
                return hi_w, thr_w, cnt_w, unfinished(cnt_w)

            thr0 = max_below(hi)
            cnt0 = count_ge(thr0)
            _, thr_w, cnt_w, _ = lax.while_loop(lambda carry: carry[3], step, (hi, thr0, cnt0, unfinished(cnt0)))
            thr_w = jnp.where(pending, thr_w, lo)

            excess = jnp.logical_and(pending, cnt_w > kf)

            @pl.when(any_true(excess))
            def _():
                need = jnp.where(excess, kf - count(lambda x, c: x > thr_w), float(score_ref.shape[0]))
                r_i = lax.broadcasted_iota(jnp.int32, (K_CHUNK, K_CHUNK), 0)
                c_i = lax.broadcasted_iota(jnp.int32, (K_CHUNK, K_CHUNK), 1)
                earlier = (c_i < r_i).astype(BF16)

                def demote(j, before):
                    c = n_chunks - 1 - j
                    x = score_ref[chunk_rows(c), :]
                    tied = jnp.where(x == thr_w, 1.0, 0.0)
                    rank = before + jnp.dot(earlier, tied.astype(BF16), preferred_element_type=F32)
                    drop = jnp.logical_and(x == thr_w, rank >= need)
                    score_ref[chunk_rows(c), :] = jnp.where(drop, -jnp.inf, x)
                    return before + jnp.sum(_fold(tied, jnp.sum), axis=0, keepdims=True)

                lax.fori_loop(0, n_chunks, demote, jnp.zeros((1, Q_TILE), F32))

            return thr_w

        thr = lax.cond(any_true(pending), walk_down, lambda: lo)
        return jnp.where(active, thr, floor)

    thr = lax.cond(t0 + Q_TILE > topk, search, lambda: floor)

    aq_pairs = pair_rows(aq_ref[...])
    near = K_CHUNK - 2 * Q_TILE
    m_ref[...] = jnp.full(m_ref.shape, NEG_BIG, F32)
    acc_ref[...] = jnp.zeros(acc_ref.shape, F32)
    ones_rows = jnp.ones((SUM_ROWS, K_CHUNK), BF16)

    def logits(c, with_bias, dst_ref):
        start = key_start(c)
        madd = jnp.where(score_ref[chunk_rows(c), :] >= thr, 0.0, NEG_BIG)
        for p in range(n_pairs):
            kp = ak_ref[pl.ds(start, K_CHUNK), p * LANES:(p + 1) * LANES]
            s = lax.dot_general(kp, aq_pairs[p], _NT, preferred_element_type=F32)
            for e in range(2):
                se = s[:, e * Q_TILE:(e + 1) * Q_TILE] + madd
                if with_bias and near:
                    se = jnp.concatenate([se[:near], se[near:] + tb_ref[2 * p + e]], axis=0)
                elif with_bias:
                    se = se + tb_ref[2 * p + e]
                dst_ref[p, :, e * Q_TILE:(e + 1) * Q_TILE] = se

    def update(c, src_ref):
        start = key_start(c)
        for p in range(n_pairs):
            vt = jnp.concatenate([avt_ref[p * LANES:(p + 1) * LANES, pl.ds(start, K_CHUNK)], ones_rows], axis=0)
            s = src_ref[p]
            m_old = m_ref[p]
            m_new = jnp.maximum(m_old, jnp.max(_fold(s, jnp.max), axis=0, keepdims=True))
            alpha = jnp.exp2(m_old - m_new)
            pe = jnp.exp2(s - m_new)
            acc_ref[p] = alpha * acc_ref[p] + jnp.dot(vt, pe.astype(BF16), preferred_element_type=F32)
            m_ref[p] = m_new

    def step(c, cur_ref, next_ref):
        logits(c + 1, False, next_ref)
        update(c, cur_ref)

    def pipelined(c, _):
        lax.cond(c % 2 == 0, lambda: step(c, la_ref, lb_ref), lambda: step(c, lb_ref, la_ref))
        return 0

    logits(0, True, la_ref)
    lax.fori_loop(0, n_chunks - 1, pipelined, 0)
    last = n_chunks - 1
    lax.cond(last % 2 == 0, lambda: update(last, la_ref), lambda: update(last, lb_ref))

    for p in range(n_pairs):
        acc = acc_ref[p]
        inv = 1.0 / acc[LANES:LANES + 1, :]
        o = jnp.concatenate([acc[:A_DH, :Q_TILE] * inv[:, :Q_TILE], acc[A_DH:LANES, Q_TILE:] * inv[:, Q_TILE:]],
                            axis=0)
        out_ref[:, p * LANES:(p + 1) * LANES] = o.T.astype(BF16)


def _dsa(iq, gw, aq, kn, ak, avt, tb, topk):
    bsz, s, _ = iq.shape
    sp = s + K_PAD
    n_pairs = A_HEADS // 2
    qblk = lambda width: pl.BlockSpec((None, Q_TILE, width), lambda b, i: (b, i, 0))
    kblk = lambda width: pl.BlockSpec((None, sp, width), lambda b, i: (b, 0, 0))
    return pl.pallas_call(
        functools.partial(_dsa_kernel, topk=topk),
        grid=(bsz, s // Q_TILE),
        in_specs=[qblk(A_WIDTH), qblk(LANES), qblk(A_WIDTH), kblk(LANES), kblk(A_WIDTH),
                  pl.BlockSpec((None, A_WIDTH, sp), lambda b, i: (b, 0, 0)),
                  pl.BlockSpec(tb.shape, lambda b, i: (0, 0, 0))],
        out_specs=qblk(A_WIDTH),
        out_shape=jax.ShapeDtypeStruct((bsz, s, A_WIDTH), BF16),
        scratch_shapes=[pltpu.VMEM((s, Q_TILE), F32),
                        pltpu.VMEM((n_pairs, K_CHUNK, 2 * Q_TILE), F32),
                        pltpu.VMEM((n_pairs, K_CHUNK, 2 * Q_TILE), F32),
                        pltpu.VMEM((n_pairs, LANES + SUM_ROWS, 2 * Q_TILE), F32),
                        pltpu.VMEM((n_pairs, 1, 2 * Q_TILE), F32)],
        compiler_params=pltpu.CompilerParams(dimension_semantics=("parallel", "arbitrary"),
                                             vmem_limit_bytes=VMEM_LIMIT),
        name="dsa",
    )(iq, gw, aq, kn, ak, avt, tb)


def _rel_bias_table(rel_bias):
    c = jnp.arange(2 * Q_TILE, dtype=jnp.int32)[:, None]
    a = jnp.arange(Q_TILE, dtype=jnp.int32)[None, :]
    dist = Q_TILE + a - c
    max_exact = REL_BUCKETS // 2
    d = jnp.maximum(dist, 0)
    large = max_exact + (jnp.log(jnp.maximum(d, 1).astype(F32) / max_exact)
                         / math.log(REL_MAX_DIST / max_exact) * (REL_BUCKETS - max_exact)).astype(jnp.int32)
    large = jnp.minimum(large, REL_BUCKETS - 1)
    bucket = jnp.where(d < max_exact, d, large)
    rb = rel_bias.astype(F32)
    onehot = (bucket[:, :, None] == jnp.arange(REL_BUCKETS, dtype=jnp.int32)).astype(F32)
    table = jnp.einsum("cak,kh->hca", onehot, rb - rb[REL_BUCKETS - 1], precision=lax.Precision.HIGHEST)
    return table * LOG2E


def _ffn_kernel(x_ref, hm_ref, ha_ref, xh_ref, hmh_ref, hah_ref, wo_ref, gf_ref,
                wg_ref, wv_ref, cwg_ref, cwv_ref, cbg_ref, cbv_ref, wd_ref, g_ref,
                out_ref, hext_ref, u_ref, a_ref, *, tiles_per_seq):
    tm = ROW_TILE

    def mixed(x, hm, ha):
        return (x + jnp.dot(hm, wo_ref[0:M_WIDTH, :], preferred_element_type=F32)
                + jnp.dot(ha, wo_ref[M_WIDTH:, :], preferred_element_type=F32))

    x1 = mixed(x_ref[...], hm_ref[...], ha_ref[...])
    halo = _rms(mixed(xh_ref[...], hmh_ref[...], hah_ref[...]), gf_ref[...]).astype(BF16)
    first = (pl.program_id(0) % tiles_per_seq) == 0
    hext_ref[0:FFN_HALO, :] = jnp.where(first, jnp.zeros_like(halo), halo)
    hext_ref[FFN_HALO:, :] = _rms(x1, gf_ref[...]).astype(BF16)
    hext = hext_ref[...]
    n_ff = wg_ref.shape[0]

    def conv(w_ref, cw_ref, cb_ref, c):
        u_ref[...] = jnp.dot(hext, w_ref[c], preferred_element_type=F32)
        y = cb_ref[c] + u_ref[FFN_HALO:, :] * cw_ref[c, FFN_CONV - 1:FFN_CONV, :]
        for j in range(FFN_CONV - 1):
            shift = FFN_CONV - 1 - j
            y = y + u_ref[FFN_HALO - shift:FFN_HALO - shift + tm, :] * cw_ref[c, j:j + 1, :]
        return y

    for c in range(n_ff):
        gate = conv(wg_ref, cwg_ref, cbg_ref, c)
        val = conv(wv_ref, cwv_ref, cbv_ref, c)
        a_ref[:, c * FF_CHUNK:(c + 1) * FF_CHUNK] = (gate * _sigmoid(gate) * val).astype(BF16)

    x2 = x1 + jnp.dot(a_ref[...], wd_ref[...], preferred_element_type=F32)
    out_ref[...] = _rms(x2, g_ref[...])


def _ffn(x2d, hm, ha, wo, gf, wg, wv, cwg, cwv, cbg, cbv, wd, g, tiles_per_seq):
    t, d = x2d.shape
    n_ff = wg.shape[0]
    row = lambda i: (i, 0)
    halo = lambda i: (jnp.maximum(i * (ROW_TILE // FFN_HALO) - 1, 0), 0)
    fixed2 = lambda i: (0, 0)
    fixed3 = lambda i: (0, 0, 0)
    tile = lambda width: pl.BlockSpec((ROW_TILE, width), row)
    lead = lambda width: pl.BlockSpec((FFN_HALO, width), halo)
    return pl.pallas_call(
        functools.partial(_ffn_kernel, tiles_per_seq=tiles_per_seq),
        grid=(t // ROW_TILE,),
        in_specs=[tile(d), tile(M_WIDTH), tile(A_WIDTH), lead(d), lead(M_WIDTH), lead(A_WIDTH),
                  pl.BlockSpec(wo.shape, fixed2), pl.BlockSpec((1, d), fixed2),
                  pl.BlockSpec(wg.shape, fixed3), pl.BlockSpec(wv.shape, fixed3),
                  pl.BlockSpec(cwg.shape, fixed3), pl.BlockSpec(cwv.shape, fixed3),
                  pl.BlockSpec(cbg.shape, fixed3), pl.BlockSpec(cbv.shape, fixed3),
                  pl.BlockSpec(wd.shape, fixed2), pl.BlockSpec((1, d), fixed2)],
        out_specs=pl.BlockSpec((ROW_TILE, d), row),
        out_shape=jax.ShapeDtypeStruct((t, d), F32),
        scratch_shapes=[pltpu.VMEM((ROW_TILE + FFN_HALO, d), BF16),
                        pltpu.VMEM((ROW_TILE + FFN_HALO, FF_CHUNK), F32),
                        pltpu.VMEM((ROW_TILE, n_ff * FF_CHUNK), BF16)],
        compiler_params=pltpu.CompilerParams(dimension_semantics=("parallel",), vmem_limit_bytes=VMEM_LIMIT),
        name="ffn",
    )(x2d, hm, ha, x2d, hm, ha, wo, gf, wg, wv, cwg, cwv, cbg, cbv, wd, g)


def _forward(x, norm_mix, w_in, conv_w, conv_b, i_bias, f_bias, mlstm_norm, idx_k_norm, rel_bias,
             w_out, norm_ffn, w_up, ffn_conv_w, ffn_conv_b, w_down, norm_final):
    bsz, s, d = x.shape
    t = bsz * s
    assert t % ROW_TILE == 0 and s % ROW_TILE == 0 and s % M_CHUNK == 0 and s % K_CHUNK == 0
    topk = min(TOPK_MAX, s // 4)

    sizes = (M_HEADS * M_DQK, M_HEADS * M_DQK, M_WIDTH, M_WIDTH, M_HEADS, M_HEADS,
             A_WIDTH, A_WIDTH, A_WIDTH, IDX_HEADS * IDX_DIM, IDX_DIM, IDX_HEADS)
    offs = [0]
    for sz in sizes:
        offs.append(offs[-1] + sz)
    col = lambda k: w_in[:, offs[k]:offs[k + 1]]
    small_pad = jnp.zeros((d, LANES - 2 * M_HEADS - IDX_HEADS), w_in.dtype)
    w_cat = jnp.concatenate([col(0), col(1), col(2), col(3), col(6) * (A_DH ** -0.5 * LOG2E), col(7), col(8),
                             col(9) * (IDX_DIM ** -0.5), col(10), col(10), col(4), col(5), col(11), small_pad],
                            axis=1).astype(BF16)
    gk = jnp.concatenate([idx_k_norm, idx_k_norm]).reshape(1, LANES).astype(F32)

    qk, mv, mo, aq, iq, gw, ak, avt, kn = _in_proj(x, norm_mix.reshape(1, d).astype(F32), w_cat, gk)

    gate_bias = jnp.concatenate([i_bias, f_bias, jnp.zeros((LANES - 2 * M_HEADS,), F32)]).reshape(1, LANES)
    hm = _mlstm(qk, mv, mo, gw, conv_w.astype(F32), conv_b.reshape(1, -1).astype(F32),
                gate_bias, mlstm_norm.reshape(1, -1).astype(F32))
    ha = _dsa(iq, gw, aq, kn, ak, avt, _rel_bias_table(rel_bias), topk)

    d_ff = w_down.shape[0]
    n_ff = d_ff // FF_CHUNK
    assert d_ff % FF_CHUNK == 0
    chunks = lambda w: jnp.transpose(w.reshape(w.shape[0], n_ff, FF_CHUNK), (1, 0, 2))
    wg = chunks(w_up[:, :d_ff]).astype(BF16)
    wv = chunks(w_up[:, d_ff:]).astype(BF16)
    cwg = chunks(ffn_conv_w[:, :d_ff]).astype(F32)
    cwv = chunks(ffn_conv_w[:, d_ff:]).astype(F32)
    cbg = ffn_conv_b[:d_ff].reshape(n_ff, 1, FF_CHUNK).astype(F32)
    cbv = ffn_conv_b[d_ff:].reshape(n_ff, 1, FF_CHUNK).astype(F32)
    out = _ffn(x.reshape(t, d), hm.reshape(t, M_WIDTH), ha.reshape(t, A_WIDTH), w_out.astype(BF16),
               norm_ffn.reshape(1, d).astype(F32), wg, wv, cwg, cwv, cbg, cbv, w_down.astype(BF16),
               norm_final.reshape(1, d).astype(F32), s // ROW_TILE)
    return hm, ha, out.reshape(bsz, s, d)


def kernel(x, norm_mix, w_in, mlstm_conv_w, mlstm_conv_b, i_bias, f_bias, mlstm_norm, idx_k_norm, rel_bias,
           w_out, norm_ffn, w_up, ffn_conv_w, ffn_conv_b, w_down, norm_final):
    assert norm_mix.shape[0] == 1, "single-layer operation"
    return _forward(x, norm_mix[0], w_in[0], mlstm_conv_w[0], mlstm_conv_b[0], i_bias[0], f_bias[0],
                    mlstm_norm[0], idx_k_norm[0], rel_bias, w_out[0], norm_ffn[0], w_up[0], ffn_conv_w[0],
                    ffn_conv_b[0], w_down[0], norm_final)[2]
```

```python
import functools
import math

import jax
import jax.numpy as jnp
from jax import lax
from jax.experimental import pallas as pl
from jax.experimental.pallas import tpu as pltpu

F32 = jnp.float32
BF16 = jnp.bfloat16

EPS = 1e-6
LANES = 128
SUBLANES = 8

M_HEADS, M_DV, M_DQK, M_CONV = 4, 128, 64, 4
A_HEADS, A_DH = 8, 64
IDX_HEADS, IDX_DIM = 8, 64
TOPK_MAX = 256
REL_BUCKETS, REL_MAX_DIST = 32, 128
FFN_CONV = 3
M_WIDTH = M_HEADS * M_DV
A_WIDTH = A_HEADS * A_DH
QK_WIDTH = 2 * M_HEADS * M_DQK
LOG2E = math.log2(math.e)

ROW_TILE = 512
M_CHUNK = 256
Q_TILE = 128
K_CHUNK = 512
K_PAD = ROW_TILE
FOLD = 4 * SUBLANES
SUM_ROWS = 2 * SUBLANES
FF_CHUNK = 256
FFN_HALO = 16
BISECT_MIN = 15
BISECT_GROUP = 3
BISECT_MAX = 30
BISECT_STALL = 2
NEG_BIG = -1e30
VMEM_LIMIT = 56 * 1024 * 1024

_NT = (((1,), (1,)), ((), ()))
_TN = (((0,), (0,)), ((), ()))


def _rms(x, g):
    return x * lax.rsqrt(jnp.mean(x * x, axis=-1, keepdims=True) + EPS) * g


def _sigmoid(x):
    return 1.0 / (1.0 + jnp.exp(-x))


def _log_sigmoid(x):
    return jnp.minimum(x, 0.0) - jnp.log(1.0 + jnp.exp(-jnp.abs(x)))


def _in_proj_kernel(x_ref, g_ref, w_ref, gk_ref,
                    qk_ref, mv_ref, mo_ref, aq_ref, iq_ref, gw_ref, ak_ref, avt_ref, kn_ref):
    @pl.when(pl.program_id(1) == 0)
    def _():
        ak_ref[...] = jnp.zeros_like(ak_ref)
        avt_ref[...] = jnp.zeros_like(avt_ref)
        kn_ref[...] = jnp.zeros_like(kn_ref)

    @pl.when(pl.program_id(1) > 0)
    def _():
        h = _rms(x_ref[...], g_ref[...]).astype(BF16)

        def proj(col, width):
            return jnp.dot(h, w_ref[:, col:col + width], preferred_element_type=F32)

        qk_ref[...] = proj(0, 512)
        mv_ref[...] = proj(512, 512).T.astype(BF16)
        mo_ref[...] = proj(1024, 512).T
        aq_ref[...] = proj(1536, 512).astype(BF16)
        ak_ref[...] = proj(2048, 512).astype(BF16)
        avt_ref[...] = proj(2560, 512).T.astype(BF16)
        iq_ref[...] = proj(3072, 512).astype(BF16)
        kn_ref[...] = _rms(proj(3584, LANES), gk_ref[...]).astype(BF16)
        gw_ref[...] = proj(3712, LANES)


def _in_proj(x, g, w, gk):
    bsz, s, d = x.shape
    assert K_PAD == ROW_TILE
    row = lambda b, j: (b, jnp.maximum(j - 1, 0), 0)
    padded = lambda b, j: (b, j, 0)
    fixed = lambda b, j: (0, 0)
    blk = lambda width, index: pl.BlockSpec((None, ROW_TILE, width), index)
    tblk = pl.BlockSpec((None, 512, ROW_TILE), lambda b, j: (b, 0, jnp.maximum(j - 1, 0)))
    wide = lambda dt: jax.ShapeDtypeStruct((bsz, s, 512), dt)
    tall = lambda dt: jax.ShapeDtypeStruct((bsz, 512, s), dt)
    return pl.pallas_call(
        _in_proj_kernel,
        grid=(bsz, s // ROW_TILE + 1),
        in_specs=[blk(d, row), pl.BlockSpec((1, d), fixed), pl.BlockSpec(w.shape, fixed),
                  pl.BlockSpec((1, LANES), fixed)],
        out_specs=[blk(512, row), tblk, tblk, blk(512, row), blk(512, row), blk(LANES, row), blk(512, padded),
                   pl.BlockSpec((None, 512, ROW_TILE), lambda b, j: (b, 0, j)), blk(LANES, padded)],
        out_shape=[wide(F32), tall(BF16), tall(F32), wide(BF16), wide(BF16),
                   jax.ShapeDtypeStruct((bsz, s, LANES), F32),
                   jax.ShapeDtypeStruct((bsz, s + K_PAD, 512), BF16),
                   jax.ShapeDtypeStruct((bsz, 512, s + K_PAD), BF16),
                   jax.ShapeDtypeStruct((bsz, s + K_PAD, LANES), BF16)],
        compiler_params=pltpu.CompilerParams(dimension_semantics=("parallel", "arbitrary"),
                                             vmem_limit_bytes=VMEM_LIMIT),
        name="in_proj",
    )(x, g, w, gk)


def _mlstm_kernel(qk_ref, vt_ref, ot_ref, gw_ref, cw_ref, cb_ref, gb_ref, gn_ref,
                  out_ref, xp_ref, ct_ref, n_ref, m_ref):
    L = M_CHUNK

    @pl.when(pl.program_id(1) == 0)
    def _():
        xp_ref[0:SUBLANES, :] = jnp.zeros((SUBLANES, QK_WIDTH), F32)
        ct_ref[...] = jnp.zeros_like(ct_ref)
        n_ref[...] = jnp.zeros_like(n_ref)
        m_ref[...] = jnp.zeros_like(m_ref)

    x = qk_ref[...]
    xp_ref[SUBLANES:, :] = x
    y = cb_ref[...] + x * cw_ref[M_CONV - 1:M_CONV, :]
    for j in range(M_CONV - 1):
        shift = M_CONV - 1 - j
        y = y + xp_ref[SUBLANES - shift:SUBLANES - shift + L, :] * cw_ref[j:j + 1, :]
    xp_ref[0:SUBLANES, :] = x[L - SUBLANES:, :]
    y = y * _sigmoid(y)
    q_all = y[:, :M_HEADS * M_DQK] * (M_DQK ** -0.5)
    k_all = y[:, M_HEADS * M_DQK:]

    gw = gw_ref[...] + gb_ref[...]
    lane = lax.broadcasted_iota(jnp.int32, (L, LANES), 1)
    logf = _log_sigmoid(gw)
    r_i = lax.broadcasted_iota(jnp.int32, (L, L), 0)
    c_i = lax.broadcasted_iota(jnp.int32, (L, L), 1)
    tril = r_i >= c_i
    tril_b = tril.astype(BF16)
    hi = logf.astype(BF16)
    rem = logf - hi.astype(F32)
    mid = rem.astype(BF16)
    lo = (rem - mid.astype(F32)).astype(BF16)
    bcum = (jnp.dot(tril_b, hi, preferred_element_type=F32) + jnp.dot(tril_b, mid, preferred_element_type=F32)
            + jnp.dot(tril_b, lo, preferred_element_type=F32))
    cols = jnp.where(lane < M_HEADS, gw, bcum)
    rows = cols.T

    half = lax.broadcasted_iota(jnp.int32, (L, LANES), 1) < M_DQK
    first_row = lax.broadcasted_iota(jnp.int32, (SUBLANES, L), 0) == 0
    causal = r_i <= c_i

    def over_keys(x, op):
        return op(_fold(x, op), axis=0, keepdims=True)

    for hd in range(M_HEADS):
        pair, odd = hd // 2, hd % 2
        sel = half if odd == 0 else jnp.logical_not(half)
        qb = jnp.where(sel, q_all[:, pair * LANES:(pair + 1) * LANES], 0.0).astype(BF16)
        kb = k_all[:, pair * LANES:(pair + 1) * LANES].astype(BF16)
        vt = vt_ref[hd * M_DV:(hd + 1) * M_DV, :]
        ig_row = rows[hd:hd + 1, :]
        b_row = rows[M_HEADS + hd:M_HEADS + hd + 1, :]
        key_col = cols[:, hd:hd + 1] - cols[:, M_HEADS + hd:M_HEADS + hd + 1]
        ct_prev = ct_ref[hd]
        n_prev = n_ref[hd]
        m_prev = m_ref[hd][:, 0:1]

        dmat = jnp.where(causal, b_row + key_col, -jnp.inf)
        inter_log = b_row + m_prev
        m_t = jnp.maximum(inter_log, over_keys(dmat, jnp.max))
        sc = lax.dot_general(kb, qb, _NT, preferred_element_type=F32) * jnp.exp(dmat - m_t)
        inter_w = jnp.exp(inter_log - m_t)
        num = (inter_w * lax.dot_general(ct_prev.astype(BF16), qb, _NT, preferred_element_type=F32)
               + jnp.dot(vt, sc.astype(BF16), preferred_element_type=F32))
        nq = lax.dot_general(n_prev.astype(BF16), qb, _NT, preferred_element_type=F32)[0:1, :]
        den = inter_w * nq + over_keys(sc, jnp.sum)
        hh = num / jnp.maximum(jnp.abs(den), jnp.exp(-m_t))

        g = b_row[:, L - 1:L]
        a_row = g - b_row + ig_row
        m_loc = jnp.max(a_row, axis=-1, keepdims=True)
        wgt = jnp.exp(a_row - m_loc)
        kv = jnp.dot((vt * wgt).astype(BF16), kb, preferred_element_type=F32)
        ks = jnp.dot(jnp.where(first_row, wgt, 0.0).astype(BF16), kb, preferred_element_type=F32)
        m_new = jnp.maximum(g + m_prev, m_loc)
        decay = jnp.exp(g + m_prev - m_new)
        scale = jnp.exp(m_loc - m_new)
        ct_ref[hd] = decay * ct_prev + scale * kv
        n_ref[hd] = decay * n_prev + scale * ks
        m_ref[hd] = jnp.broadcast_to(m_new, (1, LANES))

        ms = over_keys(hh * hh, jnp.sum) * (1.0 / M_DV)
        hn = hh * lax.rsqrt(ms + EPS) * gn_ref[hd * M_DV:(hd + 1) * M_DV, :]
        gated = _sigmoid(ot_ref[hd * M_DV:(hd + 1) * M_DV, :]) * hn
        out_ref[:, hd * M_DV:(hd + 1) * M_DV] = gated.T.astype(BF16)


def _mlstm(qk, mvt, mot, gw, cw, cb, gb, gn):
    bsz, s, _ = qk.shape
    L = M_CHUNK
    blk = lambda width: pl.BlockSpec((None, L, width), lambda b, c: (b, c, 0))
    tblk = pl.BlockSpec((None, M_WIDTH, L), lambda b, c: (b, 0, c))
    fixed = lambda shape: pl.BlockSpec(shape, lambda b, c: (0, 0))
    return pl.pallas_call(
        _mlstm_kernel,
        grid=(bsz, s // L),
        in_specs=[blk(QK_WIDTH), tblk, tblk, blk(LANES),
                  fixed((M_CONV, QK_WIDTH)), fixed((1, QK_WIDTH)), fixed((1, LANES)), fixed((M_WIDTH, 1))],
        out_specs=blk(M_WIDTH),
        out_shape=jax.ShapeDtypeStruct((bsz, s, M_WIDTH), BF16),
        scratch_shapes=[pltpu.VMEM((L + SUBLANES, QK_WIDTH), F32),
                        pltpu.VMEM((M_HEADS, M_DV, LANES), F32),
                        pltpu.VMEM((M_HEADS, SUBLANES, LANES), F32),
                        pltpu.VMEM((M_HEADS, 1, LANES), F32)],
        compiler_params=pltpu.CompilerParams(dimension_semantics=("parallel", "arbitrary"),
                                             vmem_limit_bytes=VMEM_LIMIT),
        name="mlstm",
    )(qk, mvt, mot, gw, cw, cb, gb, gn)


def _fold(x, op):
    return op(x.reshape(x.shape[0] // FOLD, FOLD, x.shape[1]), axis=0)


def _dsa_kernel(iq_ref, gw_ref, aq_ref, kn_ref, ak_ref, avt_ref, tb_ref, out_ref,
                score_ref, la_ref, lb_ref, acc_ref, m_ref, *, topk):
    i = pl.program_id(1)
    blocks_per_chunk = K_CHUNK // Q_TILE
    n_chunks = i // blocks_per_chunk + 1
    n_pairs = A_HEADS // 2
    t0 = i * Q_TILE
    kf = float(topk)

    def key_start(c):
        return pl.multiple_of((i - blocks_per_chunk * c) * Q_TILE + (K_PAD - K_CHUNK + Q_TILE), Q_TILE)

    def chunk_rows(c):
        return pl.ds(pl.multiple_of(c * K_CHUNK, K_CHUNK), K_CHUNK)

    lane_w = lax.broadcasted_iota(jnp.int32, (Q_TILE, A_WIDTH), 1)
    low_half = (lane_w % LANES) < A_DH

    def pair_rows(x):
        even = jnp.where(low_half, x, jnp.zeros_like(x))
        odd = jnp.where(low_half, jnp.zeros_like(x), x)
        return [jnp.concatenate([even[:, p * LANES:(p + 1) * LANES], odd[:, p * LANES:(p + 1) * LANES]], axis=0)
                for p in range(n_pairs)]

    key_off = lax.broadcasted_iota(jnp.int32, (K_CHUNK, Q_TILE), 0)
    q_off = lax.broadcasted_iota(jnp.int32, (K_CHUNK, Q_TILE), 1)

    def key_index(c):
        return key_start(c) - K_PAD + key_off

    iq_all = jnp.concatenate(pair_rows(iq_ref[...]), axis=0)
    w_rows = gw_ref[...].T * (IDX_HEADS ** -0.5)

    def score_chunk(c, carry):
        mx, mn = carry
        kn = kn_ref[pl.ds(key_start(c), K_CHUNK), :]
        acc = jnp.zeros((K_CHUNK, Q_TILE), F32)
        isc = lax.dot_general(kn, iq_all, _NT, preferred_element_type=F32)
        for hd in range(IDX_HEADS):
            r = 2 * M_HEADS + hd
            acc = acc + jnp.maximum(isc[:, hd * Q_TILE:(hd + 1) * Q_TILE], 0.0) * w_rows[r:r + 1, :]
        key = key_index(c)
        ok = jnp.logical_and(key >= 0, key <= t0 + q_off)
        score_ref[chunk_rows(c), :] = jnp.where(ok, acc, -jnp.inf)
        mx = jnp.maximum(mx, _fold(jnp.where(ok, acc, -jnp.inf), jnp.max))
        mn = jnp.minimum(mn, _fold(jnp.where(ok, acc, jnp.inf), jnp.min))
        return mx, mn

    extremes = (jnp.full((FOLD, Q_TILE), -jnp.inf, F32), jnp.full((FOLD, Q_TILE), jnp.inf, F32))
    extremes = lax.fori_loop(0, n_chunks // 2, lambda j, ex: score_chunk(2 * j + 1, score_chunk(2 * j, ex)), extremes)
    mx, mn = lax.cond(n_chunks % 2 == 1, lambda ex: score_chunk(n_chunks - 1, ex), lambda ex: ex, extremes)
    q_max = jnp.max(mx, axis=0, keepdims=True)
    q_min = jnp.min(mn, axis=0, keepdims=True)

    def count(pred):
        def body(c, cnt):
            return cnt + _fold(jnp.where(pred(score_ref[chunk_rows(c), :], c), 1.0, 0.0), jnp.sum)
        return jnp.sum(lax.fori_loop(0, n_chunks, body, jnp.zeros((FOLD, Q_TILE), F32)), axis=0, keepdims=True)

    def count_ge(thr):
        return count(lambda x, c: x >= thr)

    def max_below(bound):
        def body(c, best):
            x = score_ref[chunk_rows(c), :]
            return jnp.maximum(best, _fold(jnp.where(x < bound, x, -jnp.inf), jnp.max))
        best = lax.fori_loop(0, n_chunks, body, jnp.full((FOLD, Q_TILE), -jnp.inf, F32))
        return jnp.max(best, axis=0, keepdims=True)

    def any_true(flag):
        return jnp.max(jnp.where(flag, 1, 0)) > 0

    t_row = t0 + lax.broadcasted_iota(jnp.int32, (1, Q_TILE), 1)
    active = (t_row + 1) > topk
    floor = jnp.full((1, Q_TILE), jnp.finfo(jnp.float32).min, F32)

    def search():
        def bisect(carry):
            lo, hi, hi_num, settled, c_lo, c_hi = carry
            mid = 0.5 * (lo + hi_num)
            cnt = count_ge(mid)
            open_ = settled == 0.0
            up = jnp.logical_and(open_, cnt >= kf)
            down = jnp.logical_and(open_, cnt < kf)
            settled = jnp.where(jnp.logical_and(open_, cnt == kf), 1.0, settled)
            return (jnp.where(up, mid, lo), jnp.where(down, mid, hi), jnp.where(down, mid, hi_num), settled,
                    jnp.where(up, cnt, c_lo), jnp.where(down, cnt, c_hi))

        state = (q_min, jnp.full((1, Q_TILE), jnp.inf, F32), q_max, jnp.where(active, 0.0, 1.0),
                 jnp.full((1, Q_TILE), float(score_ref.shape[0]), F32), jnp.zeros((1, Q_TILE), F32))
        state = lax.fori_loop(0, BISECT_MIN, lambda _, st: bisect(st), state)

        def more(carry):
            step, pending, moving = carry[0], carry[1], carry[2]
            return jnp.logical_and(jnp.logical_and(step < BISECT_MAX, pending), moving)

        def again(carry):
            st = carry[3:]
            moved = jnp.zeros((1, Q_TILE), F32)
            for g in range(BISECT_GROUP):
                new = bisect(st)
                if g >= BISECT_GROUP - BISECT_STALL:
                    sep = jnp.logical_and(st[3] == 0.0, jnp.logical_or(new[4] != st[4], new[5] != st[5]))
                    moved = jnp.where(sep, 1.0, moved)
                st = new
            code = jnp.sum(jnp.where(st[3] == 0.0, 1.0, 0.0) + (2.0 * Q_TILE) * moved)
            n_moved = jnp.floor(code / (2.0 * Q_TILE))
            return (carry[0] + BISECT_GROUP, code - (2.0 * Q_TILE) * n_moved > 0.5, n_moved > 0.5) + st

        out = lax.while_loop(more, again,
                             (jnp.int32(BISECT_MIN), any_true(state[3] == 0.0), jnp.bool_(True)) + state)
        lo, hi, _, settled = out[3:7]
        pending = settled == 0.0

        def walk_down():
            def unfinished(cnt):
                return any_true(jnp.logical_and(pending, cnt < kf))

            def step(carry):
                hi_w, thr_w, cnt_w, _ = carry
                hi_w = jnp.where(cnt_w < kf, thr_w, hi_w)
                thr_w = max_below(hi_w)
                cnt_w = count_ge(thr_w)
                return hi_w, thr_w, cnt_w, unfinished(cnt_w)

            thr0 = max_below(hi)
            cnt0 = count_ge(thr0)
            _, thr_w, cnt_w, _ = lax.while_loop(lambda carry: carry[3], step, (hi, thr0, cnt0, unfinished(cnt0)))
            thr_w = jnp.where(pending, thr_w, lo)

            excess = jnp.logical_and(pending, cnt_w > kf)

            @pl.when(any_true(excess))
            def _():
                need = jnp.where(excess, kf - count(lambda x, c: x > thr_w), float(score_ref.shape[0]))
                r_i = lax.broadcasted_iota(jnp.int32, (K_CHUNK, K_CHUNK), 0)
                c_i = lax.broadcasted_iota(jnp.int32, (K_CHUNK, K_CHUNK), 1)
                earlier = (c_i < r_i).astype(BF16)

                def demote(j, before):
                    c = n_chunks - 1 - j
                    x = score_ref[chunk_rows(c), :]
                    tied = jnp.where(x == thr_w, 1.0, 0.0)
                    rank = before + jnp.dot(earlier, tied.astype(BF16), preferred_element_type=F32)
                    drop = jnp.logical_and(x == thr_w, rank >= need)
                    score_ref[chunk_rows(c), :] = jnp.where(drop, -jnp.inf, x)
                    return before + jnp.sum(_fold(tied, jnp.sum), axis=0, keepdims=True)

                lax.fori_loop(0, n_chunks, demote, jnp.zeros((1, Q_TILE), F32))

            return thr_w

        thr = lax.cond(any_true(pending), walk_down, lambda: lo)
        return jnp.where(active, thr, floor)

    thr = lax.cond(t0 + Q_TILE > topk, search, lambda: floor)

    aq_pairs = pair_rows(aq_ref[...])
    near = K_CHUNK - 2 * Q_TILE
    m_ref[...] = jnp.full(m_ref.shape, NEG_BIG, F32)
    acc_ref[...] = jnp.zeros(acc_ref.shape, F32)
    ones_rows = jnp.ones((SUM_ROWS, K_CHUNK), BF16)

    def logits(c, with_bias, dst_ref):
        start = key_start(c)
        madd = jnp.where(score_ref[chunk_rows(c), :] >= thr, 0.0, NEG_BIG)
        for p in range(n_pairs):
            kp = ak_ref[pl.ds(start, K_CHUNK), p * LANES:(p + 1) * LANES]
            s = lax.dot_general(kp, aq_pairs[p], _NT, preferred_element_type=F32)
            for e in range(2):
                se = s[:, e * Q_TILE:(e + 1) * Q_TILE] + madd
                if with_bias and near:
                    se = jnp.concatenate([se[:near], se[near:] + tb_ref[2 * p + e]], axis=0)
                elif with_bias:
                    se = se + tb_ref[2 * p + e]
                dst_ref[p, :, e * Q_TILE:(e + 1) * Q_TILE] = se

    def update(c, src_ref):
        start = key_start(c)
        for p in range(n_pairs):
            vt = jnp.concatenate([avt_ref[p * LANES:(p + 1) * LANES, pl.ds(start, K_CHUNK)], ones_rows], axis=0)
            s = src_ref[p]
            m_old = m_ref[p]
            m_new = jnp.maximum(m_old, jnp.max(_fold(s, jnp.max), axis=0, keepdims=True))
            alpha = jnp.exp2(m_old - m_new)
            pe = jnp.exp2(s - m_new)
            acc_ref[p] = alpha * acc_ref[p] + jnp.dot(vt, pe.astype(BF16), preferred_element_type=F32)
            m_ref[p] = m_new

    def step(c, cur_ref, next_ref):
        logits(c + 1, False, next_ref)
        update(c, cur_ref)

    def pipelined(c, _):
        lax.cond(c % 2 == 0, lambda: step(c, la_ref, lb_ref), lambda: step(c, lb_ref, la_ref))
        return 0

    logits(0, True, la_ref)
    lax.fori_loop(0, n_chunks - 1, pipelined, 0)
    last = n_chunks - 1
    lax.cond(last % 2 == 0, lambda: update(last, la_ref), lambda: update(last, lb_ref))

    for p in range(n_pairs):
        acc = acc_ref[p]
        inv = 1.0 / acc[LANES:LANES + 1, :]
        o = jnp.concatenate([acc[:A_DH, :Q_TILE] * inv[:, :Q_TILE], acc[A_DH:LANES, Q_TILE:] * inv[:, Q_TILE:]],
                            axis=0)
        out_ref[:, p * LANES:(p + 1) * LANES] = o.T.astype(BF16)


def _dsa(iq, gw, aq, kn, ak, avt, tb, topk):
    bsz, s, _ = iq.shape
    sp = s + K_PAD
    n_pairs = A_HEADS // 2
    qblk = lambda width: pl.BlockSpec((None, Q_TILE, width), lambda b, i: (b, i, 0))
    kblk = lambda width: pl.BlockSpec((None, sp, width), lambda b, i: (b, 0, 0))
    return pl.pallas_call(
        functools.partial(_dsa_kernel, topk=topk),
        grid=(bsz, s // Q_TILE),
        in_specs=[qblk(A_WIDTH), qblk(LANES), qblk(A_WIDTH), kblk(LANES), kblk(A_WIDTH),
                  pl.BlockSpec((None, A_WIDTH, sp), lambda b, i: (b, 0, 0)),
                  pl.BlockSpec(tb.shape, lambda b, i: (0, 0, 0))],
        out_specs=qblk(A_WIDTH),
        out_shape=jax.ShapeDtypeStruct((bsz, s, A_WIDTH), BF16),
        scratch_shapes=[pltpu.VMEM((s, Q_TILE), F32),
                        pltpu.VMEM((n_pairs, K_CHUNK, 2 * Q_TILE), F32),
                        pltpu.VMEM((n_pairs, K_CHUNK, 2 * Q_TILE), F32),
                        pltpu.VMEM((n_pairs, LANES + SUM_ROWS, 2 * Q_TILE), F32),
                        pltpu.VMEM((n_pairs, 1, 2 * Q_TILE), F32)],
        compiler_params=pltpu.CompilerParams(dimension_semantics=("parallel", "arbitrary"),
                                             vmem_limit_bytes=VMEM_LIMIT),
        name="dsa",
    )(iq, gw, aq, kn, ak, avt, tb)


def _rel_bias_table(rel_bias):
    c = jnp.arange(2 * Q_TILE, dtype=jnp.int32)[:, None]
    a = jnp.arange(Q_TILE, dtype=jnp.int32)[None, :]
    dist = Q_TILE + a - c
    max_exact = REL_BUCKETS // 2
    d = jnp.maximum(dist, 0)
    large = max_exact + (jnp.log(jnp.maximum(d, 1).astype(F32) / max_exact)
                         / math.log(REL_MAX_DIST / max_exact) * (REL_BUCKETS - max_exact)).astype(jnp.int32)
    large = jnp.minimum(large, REL_BUCKETS - 1)
    bucket = jnp.where(d < max_exact, d, large)
    rb = rel_bias.astype(F32)
    onehot = (bucket[:, :, None] == jnp.arange(REL_BUCKETS, dtype=jnp.int32)).astype(F32)
    table = jnp.einsum("cak,kh->hca", onehot, rb - rb[REL_BUCKETS - 1], precision=lax.Precision.HIGHEST)
    return table * LOG2E


def _ffn_kernel(x_ref, hm_ref, ha_ref, xh_ref, hmh_ref, hah_ref, wo_ref, gf_ref,
                wu_ref, cw_ref, cb_ref, wd_ref, g_ref,
                out_ref, hext_ref, u_ref, a_ref, *, tiles_per_seq):
    tm = ROW_TILE

    def mixed(x, hm, ha):
        return (x + jnp.dot(hm, wo_ref[0:M_WIDTH, :], preferred_element_type=F32)
                + jnp.dot(ha, wo_ref[M_WIDTH:, :], preferred_element_type=F32))

    x1 = mixed(x_ref[...], hm_ref[...], ha_ref[...])
    halo = _rms(mixed(xh_ref[...], hmh_ref[...], hah_ref[...]), gf_ref[...]).astype(BF16)
    first = (pl.program_id(0) % tiles_per_seq) == 0
    hext_ref[0:FFN_HALO, :] = jnp.where(first, jnp.zeros_like(halo), halo)
    hext_ref[FFN_HALO:, :] = _rms(x1, gf_ref[...]).astype(BF16)
    hext = hext_ref[...]
    d_ff = wd_ref.shape[0]

    def conv(col):
        cs = slice(col, col + FF_CHUNK)
        u_ref[...] = jnp.dot(hext, wu_ref[:, cs], preferred_element_type=F32)
        y = cb_ref[:, cs] + u_ref[FFN_HALO:, :] * cw_ref[FFN_CONV - 1:FFN_CONV, cs]
        for j in range(FFN_CONV - 1):
            shift = FFN_CONV - 1 - j
            y = y + u_ref[FFN_HALO - shift:FFN_HALO - shift + tm, :] * cw_ref[j:j + 1, cs]
        return y

    for col in range(0, d_ff, FF_CHUNK):
        gate = conv(col)
        val = conv(d_ff + col)
        a_ref[:, col:col + FF_CHUNK] = (gate * _sigmoid(gate) * val).astype(BF16)

    x2 = x1 + jnp.dot(a_ref[...], wd_ref[...], preferred_element_type=F32)
    out_ref[...] = _rms(x2, g_ref[...])


def _ffn(x2d, hm, ha, wo, gf, wu, cw, cb, wd, g, tiles_per_seq):
    t, d = x2d.shape
    d_ff = wd.shape[0]
    assert d_ff % FF_CHUNK == 0 and wu.shape[1] == 2 * d_ff
    row = lambda i: (i, 0)
    halo = lambda i: (jnp.maximum(i * (ROW_TILE // FFN_HALO) - 1, 0), 0)
    fixed = lambda i: (0, 0)
    tile = lambda width: pl.BlockSpec((ROW_TILE, width), row)
    lead = lambda width: pl.BlockSpec((FFN_HALO, width), halo)
    whole = lambda a: pl.BlockSpec(a.shape, fixed)
    return pl.pallas_call(
        functools.partial(_ffn_kernel, tiles_per_seq=tiles_per_seq),
        grid=(t // ROW_TILE,),
        in_specs=[tile(d), tile(M_WIDTH), tile(A_WIDTH), lead(d), lead(M_WIDTH), lead(A_WIDTH),
                  whole(wo), whole(gf), whole(wu), whole(cw), whole(cb), whole(wd), whole(g)],
        out_specs=pl.BlockSpec((ROW_TILE, d), row),
        out_shape=jax.ShapeDtypeStruct((t, d), F32),
        scratch_shapes=[pltpu.VMEM((ROW_TILE + FFN_HALO, d), BF16),
                        pltpu.VMEM((ROW_TILE + FFN_HALO, FF_CHUNK), F32),
                        pltpu.VMEM((ROW_TILE, d_ff), BF16)],
        compiler_params=pltpu.CompilerParams(dimension_semantics=("parallel",), vmem_limit_bytes=VMEM_LIMIT),
        name="ffn",
    )(x2d, hm, ha, x2d, hm, ha, wo, gf, wu, cw, cb, wd, g)


def _forward(x, norm_mix, w_in, conv_w, conv_b, i_bias, f_bias, mlstm_norm, idx_k_norm, rel_bias,
             w_out, norm_ffn, w_up, ffn_conv_w, ffn_conv_b, w_down, norm_final):
    bsz, s, d = x.shape
    t = bsz * s
    assert t % ROW_TILE == 0 and s % ROW_TILE == 0 and s % M_CHUNK == 0 and s % K_CHUNK == 0
    topk = min(TOPK_MAX, s // 4)

    sizes = (M_HEADS * M_DQK, M_HEADS * M_DQK, M_WIDTH, M_WIDTH, M_HEADS, M_HEADS,
             A_WIDTH, A_WIDTH, A_WIDTH, IDX_HEADS * IDX_DIM, IDX_DIM, IDX_HEADS)
    offs = [0]
    for sz in sizes:
        offs.append(offs[-1] + sz)
    col = lambda k: w_in[:, offs[k]:offs[k + 1]]
    small_pad = jnp.zeros((d, LANES - 2 * M_HEADS - IDX_HEADS), w_in.dtype)
    w_cat = jnp.concatenate([col(0), col(1), col(2), col(3), col(6) * (A_DH ** -0.5 * LOG2E), col(7), col(8),
                             col(9) * (IDX_DIM ** -0.5), col(10), col(10), col(4), col(5), col(11), small_pad],
                            axis=1).astype(BF16)
    gk = jnp.concatenate([idx_k_norm, idx_k_norm]).reshape(1, LANES).astype(F32)

    qk, mv, mo, aq, iq, gw, ak, avt, kn = _in_proj(x, norm_mix.reshape(1, d).astype(F32), w_cat, gk)

    gate_bias = jnp.concatenate([i_bias, f_bias, jnp.zeros((LANES - 2 * M_HEADS,), F32)]).reshape(1, LANES)
    hm = _mlstm(qk, mv, mo, gw, conv_w.astype(F32), conv_b.reshape(1, -1).astype(F32),
                gate_bias, mlstm_norm.reshape(-1, 1).astype(F32))
    ha = _dsa(iq, gw, aq, kn, ak, avt, _rel_bias_table(rel_bias), topk)

    out = _ffn(x.reshape(t, d), hm.reshape(t, M_WIDTH), ha.reshape(t, A_WIDTH), w_out.astype(BF16),
               norm_ffn.reshape(1, d).astype(F32), w_up.astype(BF16), ffn_conv_w.astype(F32),
               ffn_conv_b.reshape(1, -1).astype(F32), w_down.astype(BF16),
               norm_final.reshape(1, d).astype(F32), s // ROW_TILE)
    return hm, ha, out.reshape(bsz, s, d)


def kernel(x, norm_mix, w_in, mlstm_conv_w, mlstm_conv_b, i_bias, f_bias, mlstm_norm, idx_k_norm, rel_bias,
           w_out, norm_ffn, w_up, ffn_conv_w, ffn_conv_b, w_down, norm_final):
    assert norm_mix.shape[0] == 1, "single-layer operation"
    return _forward(x, norm_mix[0], w_in[0], mlstm_conv_w[0], mlstm_conv_b[0], i_bias[0], f_bias[0],
                    mlstm_norm[0], idx_k_norm[0], rel_bias, w_out[0], norm_ffn[0], w_up[0], ffn_conv_w[0],
                    ffn_conv_b[0], w_down[0], norm_final)[2]
```

```python
import functools
import math

import jax
import jax.numpy as jnp
from jax import lax
from jax.experimental import pallas as pl
from jax.experimental.pallas import tpu as pltpu

F32 = jnp.float32
BF16 = jnp.bfloat16

EPS = 1e-6
LANES = 128
SUBLANES = 8

M_HEADS, M_DV, M_DQK, M_CONV = 4, 128, 64, 4
A_HEADS, A_DH = 8, 64
IDX_HEADS, IDX_DIM = 8, 64
TOPK_MAX = 256
REL_BUCKETS, REL_MAX_DIST = 32, 128
FFN_CONV = 3
M_WIDTH = M_HEADS * M_DV
A_WIDTH = A_HEADS * A_DH
QK_WIDTH = 2 * M_HEADS * M_DQK
LOG2E = math.log2(math.e)

ROW_TILE = 512
M_CHUNK = 256
Q_TILE = 128
K_CHUNK = 512
K_PAD = ROW_TILE
FOLD = 4 * SUBLANES
SUM_ROWS = 2 * SUBLANES
FF_CHUNK = 256
FFN_HALO = 16
BISECT_MIN = 18
BISECT_GROUP = 3
BISECT_MAX = 30
BISECT_STALL = 2
NEG_BIG = -1e30
VMEM_LIMIT = 56 * 1024 * 1024

_NT = (((1,), (1,)), ((), ()))

IN_WIDTHS = (QK_WIDTH, M_WIDTH, M_WIDTH, A_WIDTH, A_WIDTH, A_WIDTH, IDX_HEADS * IDX_DIM, LANES, LANES)
IN_OFFS = tuple(sum(IN_WIDTHS[:k]) for k in range(len(IN_WIDTHS) + 1))


def _rms(x, g):
    return x * lax.rsqrt(jnp.mean(x * x, axis=-1, keepdims=True) + EPS) * g


def _sigmoid(x):
    return 1.0 / (1.0 + jnp.exp(-x))


def _log_sigmoid(x):
    return jnp.minimum(x, 0.0) - jnp.log(1.0 + jnp.exp(-jnp.abs(x)))


def _in_proj_kernel(x_ref, g_ref, w_ref, gk_ref,
                    qk_ref, mv_ref, mo_ref, aq_ref, iq_ref, gw_ref, ak_ref, avt_ref, kn_ref):
    @pl.when(pl.program_id(1) == 0)
    def _():
        ak_ref[...] = jnp.zeros_like(ak_ref)
        avt_ref[...] = jnp.zeros_like(avt_ref)
        kn_ref[...] = jnp.zeros_like(kn_ref)

    @pl.when(pl.program_id(1) > 0)
    def _():
        h = _rms(x_ref[...], g_ref[...]).astype(BF16)

        def proj(group):
            return jnp.dot(h, w_ref[:, IN_OFFS[group]:IN_OFFS[group + 1]], preferred_element_type=F32)

        qk_ref[...] = proj(0)
        mv_ref[...] = proj(1).T.astype(BF16)
        mo_ref[...] = proj(2).T
        aq_ref[...] = proj(3).astype(BF16)
        ak_ref[...] = proj(4).astype(BF16)
        avt_ref[...] = proj(5).T.astype(BF16)
        iq_ref[...] = proj(6).astype(BF16)
        kn_ref[...] = _rms(proj(7), gk_ref[...]).astype(BF16)
        gw_ref[...] = proj(8)


def _in_proj(x, g, w, gk):
    bsz, s, d = x.shape
    assert K_PAD == ROW_TILE
    row = lambda b, j: (b, jnp.maximum(j - 1, 0), 0)
    padded = lambda b, j: (b, j, 0)
    fixed = lambda b, j: (0, 0)
    blk = lambda width, index: pl.BlockSpec((None, ROW_TILE, width), index)
    tblk = pl.BlockSpec((None, M_WIDTH, ROW_TILE), lambda b, j: (b, 0, jnp.maximum(j - 1, 0)))
    rows = lambda width, dt: jax.ShapeDtypeStruct((bsz, s, width), dt)
    return pl.pallas_call(
        _in_proj_kernel,
        grid=(bsz, s // ROW_TILE + 1),
        in_specs=[blk(d, row), pl.BlockSpec((1, d), fixed), pl.BlockSpec(w.shape, fixed),
                  pl.BlockSpec((1, LANES), fixed)],
        out_specs=[blk(QK_WIDTH, row), tblk, tblk, blk(A_WIDTH, row), blk(IDX_HEADS * IDX_DIM, row),
                   blk(LANES, row), blk(A_WIDTH, padded),
                   pl.BlockSpec((None, A_WIDTH, ROW_TILE), lambda b, j: (b, 0, j)), blk(LANES, padded)],
        out_shape=[rows(QK_WIDTH, F32),
                   jax.ShapeDtypeStruct((bsz, M_WIDTH, s), BF16), jax.ShapeDtypeStruct((bsz, M_WIDTH, s), F32),
                   rows(A_WIDTH, BF16), rows(IDX_HEADS * IDX_DIM, BF16), rows(LANES, F32),
                   jax.ShapeDtypeStruct((bsz, s + K_PAD, A_WIDTH), BF16),
                   jax.ShapeDtypeStruct((bsz, A_WIDTH, s + K_PAD), BF16),
                   jax.ShapeDtypeStruct((bsz, s + K_PAD, LANES), BF16)],
        compiler_params=pltpu.CompilerParams(dimension_semantics=("parallel", "arbitrary"),
                                             vmem_limit_bytes=VMEM_LIMIT),
        name="in_proj",
    )(x, g, w, gk)


def _mlstm_kernel(qk_ref, vt_ref, ot_ref, gw_ref, cw_ref, cb_ref, gb_ref, gn_ref,
                  out_ref, xp_ref, ct_ref, n_ref, m_ref):
    L = M_CHUNK

    @pl.when(pl.program_id(1) == 0)
    def _():
        xp_ref[0:SUBLANES, :] = jnp.zeros((SUBLANES, QK_WIDTH), F32)
        ct_ref[...] = jnp.zeros_like(ct_ref)
        n_ref[...] = jnp.zeros_like(n_ref)
        m_ref[...] = jnp.zeros_like(m_ref)

    x = qk_ref[...]
    xp_ref[SUBLANES:, :] = x
    y = cb_ref[...] + x * cw_ref[M_CONV - 1:M_CONV, :]
    for j in range(M_CONV - 1):
        shift = M_CONV - 1 - j
        y = y + xp_ref[SUBLANES - shift:SUBLANES - shift + L, :] * cw_ref[j:j + 1, :]
    xp_ref[0:SUBLANES, :] = x[L - SUBLANES:, :]
    y = y * _sigmoid(y)
    q_all = y[:, :M_HEADS * M_DQK] * (M_DQK ** -0.5)
    k_all = y[:, M_HEADS * M_DQK:]

    gw = gw_ref[...] + gb_ref[...]
    lane = lax.broadcasted_iota(jnp.int32, (L, LANES), 1)
    logf = _log_sigmoid(gw)
    r_i = lax.broadcasted_iota(jnp.int32, (L, L), 0)
    c_i = lax.broadcasted_iota(jnp.int32, (L, L), 1)
    tril = r_i >= c_i
    tril_b = tril.astype(BF16)
    hi = logf.astype(BF16)
    rem = logf - hi.astype(F32)
    mid = rem.astype(BF16)
    lo = (rem - mid.astype(F32)).astype(BF16)
    bcum = (jnp.dot(tril_b, hi, preferred_element_type=F32) + jnp.dot(tril_b, mid, preferred_element_type=F32)
            + jnp.dot(tril_b, lo, preferred_element_type=F32))
    cols = jnp.where(lane < M_HEADS, gw, bcum)
    rows = cols.T

    half = lax.broadcasted_iota(jnp.int32, (L, LANES), 1) < M_DQK
    first_row = lax.broadcasted_iota(jnp.int32, (SUBLANES, L), 0) == 0
    causal = r_i <= c_i

    def over_keys(x, op):
        return op(_fold(x, op), axis=0, keepdims=True)

    for hd in range(M_HEADS):
        pair, odd = hd // 2, hd % 2
        sel = half if odd == 0 else jnp.logical_not(half)
        qb = jnp.where(sel, q_all[:, pair * LANES:(pair + 1) * LANES], 0.0).astype(BF16)
        kb = k_all[:, pair * LANES:(pair + 1) * LANES].astype(BF16)
        vt = vt_ref[hd * M_DV:(hd + 1) * M_DV, :]
        ig_row = rows[hd:hd + 1, :]
        b_row = rows[M_HEADS + hd:M_HEADS + hd + 1, :]
        key_col = cols[:, hd:hd + 1] - cols[:, M_HEADS + hd:M_HEADS + hd + 1]
        ct_prev = ct_ref[hd]
        n_prev = n_ref[hd]
        m_prev = m_ref[hd][:, 0:1]

        dmat = jnp.where(causal, b_row + key_col, -jnp.inf)
        inter_log = b_row + m_prev
        m_t = jnp.maximum(inter_log, over_keys(dmat, jnp.max))
        sc = lax.dot_general(kb, qb, _NT, preferred_element_type=F32) * jnp.exp(dmat - m_t)
        inter_w = jnp.exp(inter_log - m_t)
        num = (inter_w * lax.dot_general(ct_prev.astype(BF16), qb, _NT, preferred_element_type=F32)
               + jnp.dot(vt, sc.astype(BF16), preferred_element_type=F32))
        nq = lax.dot_general(n_prev.astype(BF16), qb, _NT, preferred_element_type=F32)[0:1, :]
        den = inter_w * nq + over_keys(sc, jnp.sum)
        hh = num / jnp.maximum(jnp.abs(den), jnp.exp(-m_t))

        g = b_row[:, L - 1:L]
        a_row = g - b_row + ig_row
        m_loc = jnp.max(a_row, axis=-1, keepdims=True)
        wgt = jnp.exp(a_row - m_loc)
        kv = jnp.dot((vt * wgt).astype(BF16), kb, preferred_element_type=F32)
        ks = jnp.dot(jnp.where(first_row, wgt, 0.0).astype(BF16), kb, preferred_element_type=F32)
        m_new = jnp.maximum(g + m_prev, m_loc)
        decay = jnp.exp(g + m_prev - m_new)
        scale = jnp.exp(m_loc - m_new)
        ct_ref[hd] = decay * ct_prev + scale * kv
        n_ref[hd] = decay * n_prev + scale * ks
        m_ref[hd] = jnp.broadcast_to(m_new, (1, LANES))

        ms = over_keys(hh * hh, jnp.sum) * (1.0 / M_DV)
        hn = hh * lax.rsqrt(ms + EPS) * gn_ref[hd * M_DV:(hd + 1) * M_DV, :]
        gated = _sigmoid(ot_ref[hd * M_DV:(hd + 1) * M_DV, :]) * hn
        out_ref[:, hd * M_DV:(hd + 1) * M_DV] = gated.T.astype(BF16)


def _mlstm(qk, mvt, mot, gw, cw, cb, gb, gn):
    bsz, s, _ = qk.shape
    L = M_CHUNK
    blk = lambda width: pl.BlockSpec((None, L, width), lambda b, c: (b, c, 0))
    tblk = pl.BlockSpec((None, M_WIDTH, L), lambda b, c: (b, 0, c))
    fixed = lambda shape: pl.BlockSpec(shape, lambda b, c: (0, 0))
    return pl.pallas_call(
        _mlstm_kernel,
        grid=(bsz, s // L),
        in_specs=[blk(QK_WIDTH), tblk, tblk, blk(LANES),
                  fixed((M_CONV, QK_WIDTH)), fixed((1, QK_WIDTH)), fixed((1, LANES)), fixed((M_WIDTH, 1))],
        out_specs=blk(M_WIDTH),
        out_shape=jax.ShapeDtypeStruct((bsz, s, M_WIDTH), BF16),
        scratch_shapes=[pltpu.VMEM((L + SUBLANES, QK_WIDTH), F32),
                        pltpu.VMEM((M_HEADS, M_DV, LANES), F32),
                        pltpu.VMEM((M_HEADS, SUBLANES, LANES), F32),
                        pltpu.VMEM((M_HEADS, 1, LANES), F32)],
        compiler_params=pltpu.CompilerParams(dimension_semantics=("parallel", "arbitrary"),
                                             vmem_limit_bytes=VMEM_LIMIT),
        name="mlstm",
    )(qk, mvt, mot, gw, cw, cb, gb, gn)


def _fold(x, op):
    return op(x.reshape(x.shape[0] // FOLD, FOLD, x.shape[1]), axis=0)


def _dsa_kernel(iq_ref, gw_ref, aq_ref, kn_ref, ak_ref, avt_ref, tb_ref, out_ref,
                score_ref, la_ref, lb_ref, acc_ref, m_ref, *, topk):
    i = pl.program_id(1)
    blocks_per_chunk = K_CHUNK // Q_TILE
    n_chunks = i // blocks_per_chunk + 1
    n_pairs = A_HEADS // 2
    t0 = i * Q_TILE
    kf = float(topk)

    def key_start(c):
        return pl.multiple_of((i - blocks_per_chunk * c) * Q_TILE + (K_PAD - K_CHUNK + Q_TILE), Q_TILE)

    def chunk_rows(c):
        return pl.ds(pl.multiple_of(c * K_CHUNK, K_CHUNK), K_CHUNK)

    lane_w = lax.broadcasted_iota(jnp.int32, (Q_TILE, A_WIDTH), 1)
    low_half = (lane_w % LANES) < A_DH

    def pair_rows(x):
        even = jnp.where(low_half, x, jnp.zeros_like(x))
        odd = jnp.where(low_half, jnp.zeros_like(x), x)
        return [jnp.concatenate([even[:, p * LANES:(p + 1) * LANES], odd[:, p * LANES:(p + 1) * LANES]], axis=0)
                for p in range(n_pairs)]

    key_off = lax.broadcasted_iota(jnp.int32, (K_CHUNK, Q_TILE), 0)
    q_off = lax.broadcasted_iota(jnp.int32, (K_CHUNK, Q_TILE), 1)

    def key_index(c):
        return key_start(c) - K_PAD + key_off

    iq_all = jnp.concatenate(pair_rows(iq_ref[...]), axis=0)
    w_rows = gw_ref[...].T * (IDX_HEADS ** -0.5)

    def score_chunk(c, carry):
        mx, mn = carry
        kn = kn_ref[pl.ds(key_start(c), K_CHUNK), :]
        acc = jnp.zeros((K_CHUNK, Q_TILE), F32)
        isc = lax.dot_general(kn, iq_all, _NT, preferred_element_type=F32)
        for hd in range(IDX_HEADS):
            r = 2 * M_HEADS + hd
            acc = acc + jnp.maximum(isc[:, hd * Q_TILE:(hd + 1) * Q_TILE], 0.0) * w_rows[r:r + 1, :]
        key = key_index(c)
        ok = jnp.logical_and(key >= 0, key <= t0 + q_off)
        score_ref[chunk_rows(c), :] = jnp.where(ok, acc, -jnp.inf)
        mx = jnp.maximum(mx, _fold(jnp.where(ok, acc, -jnp.inf), jnp.max))
        mn = jnp.minimum(mn, _fold(jnp.where(ok, acc, jnp.inf), jnp.min))
        return mx, mn

    extremes = (jnp.full((FOLD, Q_TILE), -jnp.inf, F32), jnp.full((FOLD, Q_TILE), jnp.inf, F32))
    extremes = lax.fori_loop(0, n_chunks // 2, lambda j, ex: score_chunk(2 * j + 1, score_chunk(2 * j, ex)), extremes)
    mx, mn = lax.cond(n_chunks % 2 == 1, lambda ex: score_chunk(n_chunks - 1, ex), lambda ex: ex, extremes)
    q_max = jnp.max(mx, axis=0, keepdims=True)
    q_min = jnp.min(mn, axis=0, keepdims=True)

    def count(pred):
        def body(c, cnt):
            return cnt + _fold(jnp.where(pred(score_ref[chunk_rows(c), :], c), 1.0, 0.0), jnp.sum)
        return jnp.sum(lax.fori_loop(0, n_chunks, body, jnp.zeros((FOLD, Q_TILE), F32)), axis=0, keepdims=True)

    def count_ge(thr):
        return count(lambda x, c: x >= thr)

    def max_below(bound):
        def body(c, best):
            x = score_ref[chunk_rows(c), :]
            return jnp.maximum(best, _fold(jnp.where(x < bound, x, -jnp.inf), jnp.max))
        best = lax.fori_loop(0, n_chunks, body, jnp.full((FOLD, Q_TILE), -jnp.inf, F32))
        return jnp.max(best, axis=0, keepdims=True)

    def any_true(flag):
        return jnp.max(jnp.where(flag, 1, 0)) > 0

    t_row = t0 + lax.broadcasted_iota(jnp.int32, (1, Q_TILE), 1)
    active = (t_row + 1) > topk
    floor = jnp.full((1, Q_TILE), jnp.finfo(jnp.float32).min, F32)

    def search():
        def bisect(carry):
            lo, hi, hi_num, settled, c_lo, c_hi = carry
            mid = 0.5 * (lo + hi_num)
            cnt = count_ge(mid)
            open_ = settled == 0.0
            up = jnp.logical_and(open_, cnt >= kf)
            down = jnp.logical_and(open_, cnt < kf)
            settled = jnp.where(jnp.logical_and(open_, cnt == kf), 1.0, settled)
            return (jnp.where(up, mid, lo), jnp.where(down, mid, hi), jnp.where(down, mid, hi_num), settled,
                    jnp.where(up, cnt, c_lo), jnp.where(down, cnt, c_hi))

        state = (q_min, jnp.full((1, Q_TILE), jnp.inf, F32), q_max, jnp.where(active, 0.0, 1.0),
                 jnp.full((1, Q_TILE), float(score_ref.shape[0]), F32), jnp.zeros((1, Q_TILE), F32))
        state = lax.fori_loop(0, BISECT_MIN, lambda _, st: bisect(st), state)

        def more(carry):
            step, pending, moving = carry[0], carry[1], carry[2]
            return jnp.logical_and(jnp.logical_and(step < BISECT_MAX, pending), moving)

        def again(carry):
            st = carry[3:]
            moved = jnp.zeros((1, Q_TILE), F32)
            for g in range(BISECT_GROUP):
                new = bisect(st)
                if g >= BISECT_GROUP - BISECT_STALL:
                    sep = jnp.logical_and(st[3] == 0.0, jnp.logical_or(new[4] != st[4], new[5] != st[5]))
                    moved = jnp.where(sep, 1.0, moved)
                st = new
            code = jnp.sum(jnp.where(st[3] == 0.0, 1.0, 0.0) + (2.0 * Q_TILE) * moved)
            n_moved = jnp.floor(code / (2.0 * Q_TILE))
            return (carry[0] + BISECT_GROUP, code - (2.0 * Q_TILE) * n_moved > 0.5, n_moved > 0.5) + st

        out = lax.while_loop(more, again,
                             (jnp.int32(BISECT_MIN), any_true(state[3] == 0.0), jnp.bool_(True)) + state)
        lo, hi, _, settled, _, c_above = out[3:]
        pending = settled == 0.0

        def walk_down():
            def unfinished(cnt):
                return any_true(jnp.logical_and(pending, cnt < kf))

            def step(carry):
                hi_w, above, thr_w, cnt_w, _ = carry
                lower = cnt_w < kf
                hi_w = jnp.where(lower, thr_w, hi_w)
                above = jnp.where(lower, cnt_w, above)
                thr_w = max_below(hi_w)
                cnt_w = count_ge(thr_w)
                return hi_w, above, thr_w, cnt_w, unfinished(cnt_w)

            thr0 = max_below(hi)
            cnt0 = count_ge(thr0)
            _, above, thr_w, cnt_w, _ = lax.while_loop(lambda carry: carry[4], step,
                                                       (hi, c_above, thr0, cnt0, unfinished(cnt0)))
            thr_w = jnp.where(pending, thr_w, lo)

            excess = jnp.logical_and(pending, cnt_w > kf)

            @pl.when(any_true(excess))
            def _():
                need = jnp.where(excess, kf - above, float(score_ref.shape[0]))
                r_i = lax.broadcasted_iota(jnp.int32, (Q_TILE, Q_TILE), 0)
                c_i = lax.broadcasted_iota(jnp.int32, (Q_TILE, Q_TILE), 1)
                earlier = (c_i < r_i).astype(BF16)

                def demote(j, before):
                    c = n_chunks - 1 - j
                    x = score_ref[chunk_rows(c), :]
                    tied = jnp.where(x == thr_w, 1.0, 0.0)
                    ranks = []
                    for blk in range(K_CHUNK // Q_TILE):
                        t_blk = tied[blk * Q_TILE:(blk + 1) * Q_TILE]
                        ranks.append(before + jnp.dot(earlier, t_blk.astype(BF16), preferred_element_type=F32))
                        before = before + jnp.sum(_fold(t_blk, jnp.sum), axis=0, keepdims=True)
                    drop = jnp.logical_and(x == thr_w, jnp.concatenate(ranks, axis=0) >= need)
                    score_ref[chunk_rows(c), :] = jnp.where(drop, -jnp.inf, x)
                    return before

                lax.fori_loop(0, n_chunks, demote, jnp.zeros((1, Q_TILE), F32))

            return thr_w

        thr = lax.cond(any_true(pending), walk_down, lambda: lo)
        return jnp.where(active, thr, floor)

    thr = lax.cond(t0 + Q_TILE > topk, search, lambda: floor)

    aq_pairs = pair_rows(aq_ref[...])
    near = K_CHUNK - 2 * Q_TILE
    m_ref[...] = jnp.full(m_ref.shape, NEG_BIG, F32)
    acc_ref[...] = jnp.zeros(acc_ref.shape, F32)
    ones_rows = jnp.ones((SUM_ROWS, K_CHUNK), BF16)

    def logits(c, with_bias, dst_ref):
        start = key_start(c)
        madd = jnp.where(score_ref[chunk_rows(c), :] >= thr, 0.0, NEG_BIG)
        for p in range(n_pairs):
            kp = ak_ref[pl.ds(start, K_CHUNK), p * LANES:(p + 1) * LANES]
            s = lax.dot_general(kp, aq_pairs[p], _NT, preferred_element_type=F32)
            for e in range(2):
                se = s[:, e * Q_TILE:(e + 1) * Q_TILE] + madd
                if with_bias and near:
                    se = jnp.concatenate([se[:near], se[near:] + tb_ref[2 * p + e]], axis=0)
                elif with_bias:
                    se = se + tb_ref[2 * p + e]
                dst_ref[p, :, e * Q_TILE:(e + 1) * Q_TILE] = se

    def update(c, src_ref):
        start = key_start(c)
        for p in range(n_pairs):
            vt = jnp.concatenate([avt_ref[p * LANES:(p + 1) * LANES, pl.ds(start, K_CHUNK)], ones_rows], axis=0)
            s = src_ref[p]
            m_old = m_ref[p]
            m_new = jnp.maximum(m_old, jnp.max(_fold(s, jnp.max), axis=0, keepdims=True))
            alpha = jnp.exp2(m_old - m_new)
            pe = jnp.exp2(s - m_new)
            acc_ref[p] = alpha * acc_ref[p] + jnp.dot(vt, pe.astype(BF16), preferred_element_type=F32)
            m_ref[p] = m_new

    def step(c, cur_ref, next_ref):
        logits(c + 1, False, next_ref)
        update(c, cur_ref)

    def pipelined(c, _):
        lax.cond(c % 2 == 0, lambda: step(c, la_ref, lb_ref), lambda: step(c, lb_ref, la_ref))
        return 0

    logits(0, True, la_ref)
    lax.fori_loop(0, n_chunks - 1, pipelined, 0)
    last = n_chunks - 1
    lax.cond(last % 2 == 0, lambda: update(last, la_ref), lambda: update(last, lb_ref))

    for p in range(n_pairs):
        acc = acc_ref[p]
        inv = 1.0 / acc[LANES:LANES + 1, :]
        o = jnp.concatenate([acc[:A_DH, :Q_TILE] * inv[:, :Q_TILE], acc[A_DH:LANES, Q_TILE:] * inv[:, Q_TILE:]],
                            axis=0)
        out_ref[:, p * LANES:(p + 1) * LANES] = o.T.astype(BF16)


def _dsa(iq, gw, aq, kn, ak, avt, tb, topk):
    bsz, s, _ = iq.shape
    sp = s + K_PAD
    n_pairs = A_HEADS // 2
    qblk = lambda width: pl.BlockSpec((None, Q_TILE, width), lambda b, i: (b, i, 0))
    kblk = lambda width: pl.BlockSpec((None, sp, width), lambda b, i: (b, 0, 0))
    return pl.pallas_call(
        functools.partial(_dsa_kernel, topk=topk),
        grid=(bsz, s // Q_TILE),
        in_specs=[qblk(A_WIDTH), qblk(LANES), qblk(A_WIDTH), kblk(LANES), kblk(A_WIDTH),
                  pl.BlockSpec((None, A_WIDTH, sp), lambda b, i: (b, 0, 0)),
                  pl.BlockSpec(tb.shape, lambda b, i: (0, 0, 0))],
        out_specs=qblk(A_WIDTH),
        out_shape=jax.ShapeDtypeStruct((bsz, s, A_WIDTH), BF16),
        scratch_shapes=[pltpu.VMEM((s, Q_TILE), F32),
                        pltpu.VMEM((n_pairs, K_CHUNK, 2 * Q_TILE), F32),
                        pltpu.VMEM((n_pairs, K_CHUNK, 2 * Q_TILE), F32),
                        pltpu.VMEM((n_pairs, LANES + SUM_ROWS, 2 * Q_TILE), F32),
                        pltpu.VMEM((n_pairs, 1, 2 * Q_TILE), F32)],
        compiler_params=pltpu.CompilerParams(dimension_semantics=("parallel", "arbitrary"),
                                             vmem_limit_bytes=VMEM_LIMIT),
        name="dsa",
    )(iq, gw, aq, kn, ak, avt, tb)


def _rel_bias_table(rel_bias):
    c = jnp.arange(2 * Q_TILE, dtype=jnp.int32)[:, None]
    a = jnp.arange(Q_TILE, dtype=jnp.int32)[None, :]
    dist = Q_TILE + a - c
    max_exact = REL_BUCKETS // 2
    d = jnp.maximum(dist, 0)
    steps = (jnp.log(jnp.maximum(d, max_exact).astype(F32) / max_exact)
             / math.log(REL_MAX_DIST / max_exact) * (REL_BUCKETS - max_exact))
    large = jnp.minimum(max_exact + jnp.floor(steps).astype(jnp.int32), REL_BUCKETS - 1)
    bucket = jnp.where(d < max_exact, d, large)
    rb = rel_bias.astype(F32)
    onehot = (bucket[:, :, None] == jnp.arange(REL_BUCKETS, dtype=jnp.int32)).astype(F32)
    table = jnp.einsum("cak,kh->hca", onehot, rb - rb[REL_BUCKETS - 1], precision=lax.Precision.HIGHEST)
    return table * LOG2E


def _ffn_kernel(x_ref, hm_ref, ha_ref, xh_ref, hmh_ref, hah_ref, wo_ref, gf_ref,
                wu_ref, cw_ref, cb_ref, wd_ref, g_ref,
                out_ref, hext_ref, u_ref, a_ref, *, tiles_per_seq):
    tm = ROW_TILE

    def mixed(x, hm, ha):
        return (x + jnp.dot(hm, wo_ref[0:M_WIDTH, :], preferred_element_type=F32)
                + jnp.dot(ha, wo_ref[M_WIDTH:, :], preferred_element_type=F32))

    x1 = mixed(x_ref[...], hm_ref[...], ha_ref[...])
    halo = _rms(mixed(xh_ref[...], hmh_ref[...], hah_ref[...]), gf_ref[...]).astype(BF16)
    first = (pl.program_id(0) % tiles_per_seq) == 0
    hext_ref[0:FFN_HALO, :] = jnp.where(first, jnp.zeros_like(halo), halo)
    hext_ref[FFN_HALO:, :] = _rms(x1, gf_ref[...]).astype(BF16)
    hext = hext_ref[...]
    d_ff = wd_ref.shape[0]

    def conv(col):
        cs = slice(col, col + FF_CHUNK)
        u_ref[...] = jnp.dot(hext, wu_ref[:, cs], preferred_element_type=F32)
        y = cb_ref[:, cs] + u_ref[FFN_HALO:, :] * cw_ref[FFN_CONV - 1:FFN_CONV, cs]
        for j in range(FFN_CONV - 1):
            shift = FFN_CONV - 1 - j
            y = y + u_ref[FFN_HALO - shift:FFN_HALO - shift + tm, :] * cw_ref[j:j + 1, cs]
        return y

    for col in range(0, d_ff, FF_CHUNK):
        gate = conv(col)
        val = conv(d_ff + col)
        a_ref[:, col:col + FF_CHUNK] = (gate * _sigmoid(gate) * val).astype(BF16)

    x2 = x1 + jnp.dot(a_ref[...], wd_ref[...], preferred_element_type=F32)
    out_ref[...] = _rms(x2, g_ref[...])


def _ffn(x2d, hm, ha, wo, gf, wu, cw, cb, wd, g, tiles_per_seq):
    t, d = x2d.shape
    d_ff = wd.shape[0]
    assert d_ff % FF_CHUNK == 0 and wu.shape[1] == 2 * d_ff
    row = lambda i: (i, 0)
    halo = lambda i: (jnp.maximum(i * (ROW_TILE // FFN_HALO) - 1, 0), 0)
    fixed = lambda i: (0, 0)
    tile = lambda width: pl.BlockSpec((ROW_TILE, width), row)
    lead = lambda width: pl.BlockSpec((FFN_HALO, width), halo)
    whole = lambda a: pl.BlockSpec(a.shape, fixed)
    return pl.pallas_call(
        functools.partial(_ffn_kernel, tiles_per_seq=tiles_per_seq),
        grid=(t // ROW_TILE,),
        in_specs=[tile(d), tile(M_WIDTH), tile(A_WIDTH), lead(d), lead(M_WIDTH), lead(A_WIDTH),
                  whole(wo), whole(gf), whole(wu), whole(cw), whole(cb), whole(wd), whole(g)],
        out_specs=pl.BlockSpec((ROW_TILE, d), row),
        out_shape=jax.ShapeDtypeStruct((t, d), F32),
        scratch_shapes=[pltpu.VMEM((ROW_TILE + FFN_HALO, d), BF16),
                        pltpu.VMEM((ROW_TILE + FFN_HALO, FF_CHUNK), F32),
                        pltpu.VMEM((ROW_TILE, d_ff), BF16)],
        compiler_params=pltpu.CompilerParams(dimension_semantics=("parallel",), vmem_limit_bytes=VMEM_LIMIT),
        name="ffn",
    )(x2d, hm, ha, x2d, hm, ha, wo, gf, wu, cw, cb, wd, g)


def _forward(x, norm_mix, w_in, conv_w, conv_b, i_bias, f_bias, mlstm_norm, idx_k_norm, rel_bias,
             w_out, norm_ffn, w_up, ffn_conv_w, ffn_conv_b, w_down, norm_final):
    bsz, s, d = x.shape
    t = bsz * s
    assert t % ROW_TILE == 0 and s % ROW_TILE == 0 and s % M_CHUNK == 0 and s % K_CHUNK == 0
    topk = min(TOPK_MAX, s // 4)

    sizes = (M_HEADS * M_DQK, M_HEADS * M_DQK, M_WIDTH, M_WIDTH, M_HEADS, M_HEADS,
             A_WIDTH, A_WIDTH, A_WIDTH, IDX_HEADS * IDX_DIM, IDX_DIM, IDX_HEADS)
    offs = [0]
    for sz in sizes:
        offs.append(offs[-1] + sz)
    col = lambda k: w_in[:, offs[k]:offs[k + 1]]
    small_pad = jnp.zeros((d, LANES - 2 * M_HEADS - IDX_HEADS), w_in.dtype)
    w_cat = jnp.concatenate([col(0), col(1), col(2), col(3), col(6) * (A_DH ** -0.5 * LOG2E), col(7), col(8),
                             col(9) * (IDX_DIM ** -0.5), col(10), col(10), col(4), col(5), col(11), small_pad],
                            axis=1).astype(BF16)
    assert w_cat.shape[1] == IN_OFFS[-1]
    gk = jnp.concatenate([idx_k_norm, idx_k_norm]).reshape(1, LANES).astype(F32)

    qk, mv, mo, aq, iq, gw, ak, avt, kn = _in_proj(x, norm_mix.reshape(1, d).astype(F32), w_cat, gk)

    gate_bias = jnp.concatenate([i_bias, f_bias, jnp.zeros((LANES - 2 * M_HEADS,), F32)]).reshape(1, LANES)
    hm = _mlstm(qk, mv, mo, gw, conv_w.astype(F32), conv_b.reshape(1, -1).astype(F32),
                gate_bias, mlstm_norm.reshape(-1, 1).astype(F32))
    ha = _dsa(iq, gw, aq, kn, ak, avt, _rel_bias_table(rel_bias), topk)

    out = _ffn(x.reshape(t, d), hm.reshape(t, M_WIDTH), ha.reshape(t, A_WIDTH), w_out.astype(BF16),
               norm_ffn.reshape(1, d).astype(F32), w_up.astype(BF16), ffn_conv_w.astype(F32),
               ffn_conv_b.reshape(1, -1).astype(F32), w_down.astype(BF16),
               norm_final.reshape(1, d).astype(F32), s // ROW_TILE)
    return hm, ha, out.reshape(bsz, s, d)


def kernel(x, norm_mix, w_in, mlstm_conv_w, mlstm_conv_b, i_bias, f_bias, mlstm_norm, idx_k_norm, rel_bias,
           w_out, norm_ffn, w_up, ffn_conv_w, ffn_conv_b, w_down, norm_final):
    assert norm_mix.shape[0] == 1, "single-layer operation"
    return _forward(x, norm_mix[0], w_in[0], mlstm_conv_w[0], mlstm_conv_b[0], i_bias[0], f_bias[0],
                    mlstm_norm[0], idx_k_norm[0], rel_bias, w_out[0], norm_ffn[0], w_up[0], ffn_conv_w[0],
                    ffn_conv_b[0], w_down[0], norm_final)[2]
```

```python
import functools
import math

import jax
import jax.numpy as jnp
from jax import lax
from jax.experimental import pallas as pl
from jax.experimental.pallas import tpu as pltpu

F32 = jnp.float32
BF16 = jnp.bfloat16

EPS = 1e-6
LANES = 128
SUBLANES = 8

M_HEADS, M_DV, M_DQK, M_CONV = 4, 128, 64, 4
A_HEADS, A_DH = 8, 64
IDX_HEADS, IDX_DIM = 8, 64
TOPK_MAX = 256
REL_BUCKETS, REL_MAX_DIST = 32, 128
FFN_CONV = 3
M_WIDTH = M_HEADS * M_DV
A_WIDTH = A_HEADS * A_DH
QK_WIDTH = 2 * M_HEADS * M_DQK
LOG2E = math.log2(math.e)

ROW_TILE = 512
M_CHUNK = 256
M_PER_STEP = 4
Q_TILE = 128
Q_PER_STEP = 4
K_CHUNK = 512
K_PAD = ROW_TILE
FOLD = 4 * SUBLANES
SUM_ROWS = 2 * SUBLANES
FF_CHUNK = 256
FFN_HALO = 16
BISECT_MIN = 18
BISECT_GROUP = 3
BISECT_MAX = 30
BISECT_STALL = 2
NEG_BIG = -1e30
VMEM_LIMIT = 56 * 1024 * 1024

_NT = (((1,), (1,)), ((), ()))

IN_WIDTHS = (QK_WIDTH, M_WIDTH, M_WIDTH, A_WIDTH, A_WIDTH, A_WIDTH, IDX_HEADS * IDX_DIM, LANES, LANES)
IN_OFFS = tuple(sum(IN_WIDTHS[:k]) for k in range(len(IN_WIDTHS) + 1))


def _rms(x, g):
    return x * lax.rsqrt(jnp.mean(x * x, axis=-1, keepdims=True) + EPS) * g


def _sigmoid(x):
    return 1.0 / (1.0 + jnp.exp(-x))


def _log_sigmoid(x):
    return jnp.minimum(x, 0.0) - jnp.log(1.0 + jnp.exp(-jnp.abs(x)))


def _in_proj_kernel(x_ref, g_ref, w_ref, gk_ref,
                    qk_ref, mv_ref, mo_ref, aq_ref, iq_ref, gw_ref, ak_ref, avt_ref, kn_ref):
    @pl.when(pl.program_id(1) == 0)
    def _():
        ak_ref[...] = jnp.zeros_like(ak_ref)
        avt_ref[...] = jnp.zeros_like(avt_ref)
        kn_ref[...] = jnp.zeros_like(kn_ref)

    @pl.when(pl.program_id(1) > 0)
    def _():
        h = _rms(x_ref[...], g_ref[...]).astype(BF16)

        def proj(group):
            return jnp.dot(h, w_ref[:, IN_OFFS[group]:IN_OFFS[group + 1]], preferred_element_type=F32)

        qk_ref[...] = proj(0)
        mv_ref[...] = proj(1).T.astype(BF16)
        mo_ref[...] = proj(2).T
        aq_ref[...] = proj(3).astype(BF16)
        ak_ref[...] = proj(4).astype(BF16)
        avt_ref[...] = proj(5).T.astype(BF16)
        iq_ref[...] = proj(6).astype(BF16)
        kn_ref[...] = _rms(proj(7), gk_ref[...]).astype(BF16)
        gw_ref[...] = proj(8)


def _in_proj(x, g, w, gk):
    bsz, s, d = x.shape
    assert K_PAD == ROW_TILE
    row = lambda b, j: (b, jnp.maximum(j - 1, 0), 0)
    padded = lambda b, j: (b, j, 0)
    fixed = lambda b, j: (0, 0)
    blk = lambda width, index: pl.BlockSpec((None, ROW_TILE, width), index)
    tblk = pl.BlockSpec((None, M_WIDTH, ROW_TILE), lambda b, j: (b, 0, jnp.maximum(j - 1, 0)))
    rows = lambda width, dt: jax.ShapeDtypeStruct((bsz, s, width), dt)
    return pl.pallas_call(
        _in_proj_kernel,
        grid=(bsz, s // ROW_TILE + 1),
        in_specs=[blk(d, row), pl.BlockSpec((1, d), fixed), pl.BlockSpec(w.shape, fixed),
                  pl.BlockSpec((1, LANES), fixed)],
        out_specs=[blk(QK_WIDTH, row), tblk, tblk, blk(A_WIDTH, row), blk(IDX_HEADS * IDX_DIM, row),
                   blk(LANES, row), blk(A_WIDTH, padded),
                   pl.BlockSpec((None, A_WIDTH, ROW_TILE), lambda b, j: (b, 0, j)), blk(LANES, padded)],
        out_shape=[rows(QK_WIDTH, F32),
                   jax.ShapeDtypeStruct((bsz, M_WIDTH, s), BF16), jax.ShapeDtypeStruct((bsz, M_WIDTH, s), F32),
                   rows(A_WIDTH, BF16), rows(IDX_HEADS * IDX_DIM, BF16), rows(LANES, F32),
                   jax.ShapeDtypeStruct((bsz, s + K_PAD, A_WIDTH), BF16),
                   jax.ShapeDtypeStruct((bsz, A_WIDTH, s + K_PAD), BF16),
                   jax.ShapeDtypeStruct((bsz, s + K_PAD, LANES), BF16)],
        compiler_params=pltpu.CompilerParams(dimension_semantics=("parallel", "arbitrary"),
                                             vmem_limit_bytes=VMEM_LIMIT),
        name="in_proj",
    )(x, g, w, gk)


def _mlstm_kernel(qk_ref, vt_ref, ot_ref, gw_ref, cw_ref, cb_ref, gb_ref, gn_ref, out_ref, *scratch):
    def one_chunk(j, carry):
        rows = pl.ds(pl.multiple_of(j * M_CHUNK, M_CHUNK), M_CHUNK)
        _mlstm_chunk(pl.program_id(1) * M_PER_STEP + j, qk_ref.at[rows], vt_ref.at[:, rows], ot_ref.at[:, rows],
                     gw_ref.at[rows], cw_ref, cb_ref, gb_ref, gn_ref, out_ref.at[rows], *scratch)
        return carry

    lax.fori_loop(0, M_PER_STEP, one_chunk, 0)


def _mlstm_chunk(chunk, qk_ref, vt_ref, ot_ref, gw_ref, cw_ref, cb_ref, gb_ref, gn_ref,
                 out_ref, xp_ref, ct_ref, n_ref, m_ref):
    L = M_CHUNK

    @pl.when(chunk == 0)
    def _():
        xp_ref[0:SUBLANES, :] = jnp.zeros((SUBLANES, QK_WIDTH), F32)
        ct_ref[...] = jnp.zeros_like(ct_ref)
        n_ref[...] = jnp.zeros_like(n_ref)
        m_ref[...] = jnp.zeros_like(m_ref)

    x = qk_ref[...]
    xp_ref[SUBLANES:, :] = x
    y = cb_ref[...] + x * cw_ref[M_CONV - 1:M_CONV, :]
    for j in range(M_CONV - 1):
        shift = M_CONV - 1 - j
        y = y + xp_ref[SUBLANES - shift:SUBLANES - shift + L, :] * cw_ref[j:j + 1, :]
    xp_ref[0:SUBLANES, :] = x[L - SUBLANES:, :]
    y = y * _sigmoid(y)
    q_all = y[:, :M_HEADS * M_DQK] * (M_DQK ** -0.5)
    k_all = y[:, M_HEADS * M_DQK:]

    gw = gw_ref[...] + gb_ref[...]
    lane = lax.broadcasted_iota(jnp.int32, (L, LANES), 1)
    logf = _log_sigmoid(gw)
    r_i = lax.broadcasted_iota(jnp.int32, (L, L), 0)
    c_i = lax.broadcasted_iota(jnp.int32, (L, L), 1)
    tril = r_i >= c_i
    tril_b = tril.astype(BF16)
    hi = logf.astype(BF16)
    rem = logf - hi.astype(F32)
    mid = rem.astype(BF16)
    lo = (rem - mid.astype(F32)).astype(BF16)
    bcum = (jnp.dot(tril_b, hi, preferred_element_type=F32) + jnp.dot(tril_b, mid, preferred_element_type=F32)
            + jnp.dot(tril_b, lo, preferred_element_type=F32))
    cols = jnp.where(lane < M_HEADS, gw, bcum)
    rows = cols.T

    half = lax.broadcasted_iota(jnp.int32, (L, LANES), 1) < M_DQK
    first_row = lax.broadcasted_iota(jnp.int32, (SUBLANES, L), 0) == 0
    causal = r_i <= c_i

    def over_keys(x, op):
        return op(_fold(x, op), axis=0, keepdims=True)

    for hd in range(M_HEADS):
        pair, odd = hd // 2, hd % 2
        sel = half if odd == 0 else jnp.logical_not(half)
        qb = jnp.where(sel, q_all[:, pair * LANES:(pair + 1) * LANES], 0.0).astype(BF16)
        kb = k_all[:, pair * LANES:(pair + 1) * LANES].astype(BF16)
        vt = vt_ref[hd * M_DV:(hd + 1) * M_DV, :]
        ig_row = rows[hd:hd + 1, :]
        b_row = rows[M_HEADS + hd:M_HEADS + hd + 1, :]
        key_col = cols[:, hd:hd + 1] - cols[:, M_HEADS + hd:M_HEADS + hd + 1]
        ct_prev = ct_ref[hd]
        n_prev = n_ref[hd]
        m_prev = m_ref[hd][:, 0:1]

        dmat = jnp.where(causal, b_row + key_col, -jnp.inf)
        inter_log = b_row + m_prev
        m_t = jnp.maximum(inter_log, over_keys(dmat, jnp.max))
        sc = lax.dot_general(kb, qb, _NT, preferred_element_type=F32) * jnp.exp(dmat - m_t)
        inter_w = jnp.exp(inter_log - m_t)
        num = (inter_w * lax.dot_general(ct_prev.astype(BF16), qb, _NT, preferred_element_type=F32)
               + jnp.dot(vt, sc.astype(BF16), preferred_element_type=F32))
        nq = lax.dot_general(n_prev.astype(BF16), qb, _NT, preferred_element_type=F32)[0:1, :]
        den = inter_w * nq + over_keys(sc, jnp.sum)
        hh = num / jnp.maximum(jnp.abs(den), jnp.exp(-m_t))

        g = b_row[:, L - 1:L]
        a_row = g - b_row + ig_row
        m_loc = jnp.max(a_row, axis=-1, keepdims=True)
        wgt = jnp.exp(a_row - m_loc)
        kv = jnp.dot((vt * wgt).astype(BF16), kb, preferred_element_type=F32)
        ks = jnp.dot(jnp.where(first_row, wgt, 0.0).astype(BF16), kb, preferred_element_type=F32)
        m_new = jnp.maximum(g + m_prev, m_loc)
        decay = jnp.exp(g + m_prev - m_new)
        scale = jnp.exp(m_loc - m_new)
        ct_ref[hd] = decay * ct_prev + scale * kv
        n_ref[hd] = decay * n_prev + scale * ks
        m_ref[hd] = jnp.broadcast_to(m_new, (1, LANES))

        ms = over_keys(hh * hh, jnp.sum) * (1.0 / M_DV)
        hn = hh * lax.rsqrt(ms + EPS) * gn_ref[hd * M_DV:(hd + 1) * M_DV, :]
        gated = _sigmoid(ot_ref[hd * M_DV:(hd + 1) * M_DV, :]) * hn
        out_ref[:, hd * M_DV:(hd + 1) * M_DV] = gated.T.astype(BF16)


def _mlstm(qk, mvt, mot, gw, cw, cb, gb, gn):
    bsz, s, _ = qk.shape
    L = M_CHUNK
    span = L * M_PER_STEP
    assert s % span == 0
    blk = lambda width: pl.BlockSpec((None, span, width), lambda b, c: (b, c, 0))
    tblk = pl.BlockSpec((None, M_WIDTH, span), lambda b, c: (b, 0, c))
    fixed = lambda shape: pl.BlockSpec(shape, lambda b, c: (0, 0))
    return pl.pallas_call(
        _mlstm_kernel,
        grid=(bsz, s // span),
        in_specs=[blk(QK_WIDTH), tblk, tblk, blk(LANES),
                  fixed((M_CONV, QK_WIDTH)), fixed((1, QK_WIDTH)), fixed((1, LANES)), fixed((M_WIDTH, 1))],
        out_specs=blk(M_WIDTH),
        out_shape=jax.ShapeDtypeStruct((bsz, s, M_WIDTH), BF16),
        scratch_shapes=[pltpu.VMEM((L + SUBLANES, QK_WIDTH), F32),
                        pltpu.VMEM((M_HEADS, M_DV, LANES), F32),
                        pltpu.VMEM((M_HEADS, SUBLANES, LANES), F32),
                        pltpu.VMEM((M_HEADS, 1, LANES), F32)],
        compiler_params=pltpu.CompilerParams(dimension_semantics=("parallel", "arbitrary"),
                                             vmem_limit_bytes=VMEM_LIMIT),
        name="mlstm",
    )(qk, mvt, mot, gw, cw, cb, gb, gn)


def _fold(x, op):
    return op(x.reshape(x.shape[0] // FOLD, FOLD, x.shape[1]), axis=0)


def _dsa_kernel(iq_ref, gw_ref, aq_ref, kn_ref, ak_ref, avt_ref, tb_ref, out_ref, *scratch, topk):
    def one_tile(j, carry):
        rows = pl.ds(pl.multiple_of(j * Q_TILE, Q_TILE), Q_TILE)
        _dsa_tile(pl.program_id(1) * Q_PER_STEP + j, iq_ref.at[rows], gw_ref.at[rows], aq_ref.at[rows],
                  kn_ref, ak_ref, avt_ref, tb_ref, out_ref.at[rows], *scratch, topk=topk)
        return carry

    lax.fori_loop(0, Q_PER_STEP, one_tile, 0)


def _dsa_tile(i, iq_ref, gw_ref, aq_ref, kn_ref, ak_ref, avt_ref, tb_ref, out_ref,
              score_ref, la_ref, lb_ref, acc_ref, m_ref, *, topk):
    blocks_per_chunk = K_CHUNK // Q_TILE
    n_chunks = i // blocks_per_chunk + 1
    n_pairs = A_HEADS // 2
    t0 = i * Q_TILE
    kf = float(topk)

    def key_start(c):
        return pl.multiple_of((i - blocks_per_chunk * c) * Q_TILE + (K_PAD - K_CHUNK + Q_TILE), Q_TILE)

    def chunk_rows(c):
        return pl.ds(pl.multiple_of(c * K_CHUNK, K_CHUNK), K_CHUNK)

    lane_w = lax.broadcasted_iota(jnp.int32, (Q_TILE, A_WIDTH), 1)
    low_half = (lane_w % LANES) < A_DH

    def pair_rows(x):
        even = jnp.where(low_half, x, jnp.zeros_like(x))
        odd = jnp.where(low_half, jnp.zeros_like(x), x)
        return [jnp.concatenate([even[:, p * LANES:(p + 1) * LANES], odd[:, p * LANES:(p + 1) * LANES]], axis=0)
                for p in range(n_pairs)]

    key_off = lax.broadcasted_iota(jnp.int32, (K_CHUNK, Q_TILE), 0)
    q_off = lax.broadcasted_iota(jnp.int32, (K_CHUNK, Q_TILE), 1)

    def key_index(c):
        return key_start(c) - K_PAD + key_off

    iq_all = jnp.concatenate(pair_rows(iq_ref[...]), axis=0)
    w_rows = gw_ref[...].T * (IDX_HEADS ** -0.5)

    def score_chunk(c, carry):
        mx, mn = carry
        kn = kn_ref[pl.ds(key_start(c), K_CHUNK), :]
        acc = jnp.zeros((K_CHUNK, Q_TILE), F32)
        isc = lax.dot_general(kn, iq_all, _NT, preferred_element_type=F32)
        for hd in range(IDX_HEADS):
            r = 2 * M_HEADS + hd
            acc = acc + jnp.maximum(isc[:, hd * Q_TILE:(hd + 1) * Q_TILE], 0.0) * w_rows[r:r + 1, :]
        key = key_index(c)
        ok = jnp.logical_and(key >= 0, key <= t0 + q_off)
        score_ref[chunk_rows(c), :] = jnp.where(ok, acc, -jnp.inf)
        mx = jnp.maximum(mx, _fold(jnp.where(ok, acc, -jnp.inf), jnp.max))
        mn = jnp.minimum(mn, _fold(jnp.where(ok, acc, jnp.inf), jnp.min))
        return mx, mn

    extremes = (jnp.full((FOLD, Q_TILE), -jnp.inf, F32), jnp.full((FOLD, Q_TILE), jnp.inf, F32))
    extremes = lax.fori_loop(0, n_chunks // 2, lambda j, ex: score_chunk(2 * j + 1, score_chunk(2 * j, ex)), extremes)
    mx, mn = lax.cond(n_chunks % 2 == 1, lambda ex: score_chunk(n_chunks - 1, ex), lambda ex: ex, extremes)
    q_max = jnp.max(mx, axis=0, keepdims=True)
    q_min = jnp.min(mn, axis=0, keepdims=True)

    def count(pred):
        def body(c, cnt):
            return cnt + _fold(jnp.where(pred(score_ref[chunk_rows(c), :], c), 1.0, 0.0), jnp.sum)
        return jnp.sum(lax.fori_loop(0, n_chunks, body, jnp.zeros((FOLD, Q_TILE), F32)), axis=0, keepdims=True)

    def count_ge(thr):
        return count(lambda x, c: x >= thr)

    def max_below(bound):
        def body(c, best):
            x = score_ref[chunk_rows(c), :]
            return jnp.maximum(best, _fold(jnp.where(x < bound, x, -jnp.inf), jnp.max))
        best = lax.fori_loop(0, n_chunks, body, jnp.full((FOLD, Q_TILE), -jnp.inf, F32))
        return jnp.max(best, axis=0, keepdims=True)

    def any_true(flag):
        return jnp.max(jnp.where(flag, 1, 0)) > 0

    t_row = t0 + lax.broadcasted_iota(jnp.int32, (1, Q_TILE), 1)
    active = (t_row + 1) > topk
    floor = jnp.full((1, Q_TILE), jnp.finfo(jnp.float32).min, F32)

    def search():
        def bisect(carry):
            lo, hi, hi_num, settled, c_lo, c_hi = carry
            mid = 0.5 * (lo + hi_num)
            cnt = count_ge(mid)
            open_ = settled == 0.0
            up = jnp.logical_and(open_, cnt >= kf)
            down = jnp.logical_and(open_, cnt < kf)
            settled = jnp.where(jnp.logical_and(open_, cnt == kf), 1.0, settled)
            return (jnp.where(up, mid, lo), jnp.where(down, mid, hi), jnp.where(down, mid, hi_num), settled,
                    jnp.where(up, cnt, c_lo), jnp.where(down, cnt, c_hi))

        state = (q_min, jnp.full((1, Q_TILE), jnp.inf, F32), q_max, jnp.where(active, 0.0, 1.0),
                 jnp.full((1, Q_TILE), float(score_ref.shape[0]), F32), jnp.zeros((1, Q_TILE), F32))
        state = lax.fori_loop(0, BISECT_MIN, lambda _, st: bisect(st), state)

        def more(carry):
            step, pending, moving = carry[0], carry[1], carry[2]
            return jnp.logical_and(jnp.logical_and(step < BISECT_MAX, pending), moving)

        def again(carry):
            st = carry[3:]
            moved = jnp.zeros((1, Q_TILE), F32)
            for g in range(BISECT_GROUP):
                new = bisect(st)
                if g >= BISECT_GROUP - BISECT_STALL:
                    sep = jnp.logical_and(st[3] == 0.0, jnp.logical_or(new[4] != st[4], new[5] != st[5]))
                    moved = jnp.where(sep, 1.0, moved)
                st = new
            code = jnp.sum(jnp.where(st[3] == 0.0, 1.0, 0.0) + (2.0 * Q_TILE) * moved)
            n_moved = jnp.floor(code / (2.0 * Q_TILE))
            return (carry[0] + BISECT_GROUP, code - (2.0 * Q_TILE) * n_moved > 0.5, n_moved > 0.5) + st

        out = lax.while_loop(more, again,
                             (jnp.int32(BISECT_MIN), any_true(state[3] == 0.0), jnp.bool_(True)) + state)
        lo, hi, _, settled, _, c_above = out[3:]
        pending = settled == 0.0

        def walk_down():
            def unfinished(cnt):
                return any_true(jnp.logical_and(pending, cnt < kf))

            def step(carry):
                hi_w, above, thr_w, cnt_w, _ = carry
                lower = cnt_w < kf
                hi_w = jnp.where(lower, thr_w, hi_w)
                above = jnp.where(lower, cnt_w, above)
                thr_w = max_below(hi_w)
                cnt_w = count_ge(thr_w)
                return hi_w, above, thr_w, cnt_w, unfinished(cnt_w)

            thr0 = max_below(hi)
            cnt0 = count_ge(thr0)
            _, above, thr_w, cnt_w, _ = lax.while_loop(lambda carry: carry[4], step,
                                                       (hi, c_above, thr0, cnt0, unfinished(cnt0)))
            thr_w = jnp.where(pending, thr_w, lo)

            excess = jnp.logical_and(pending, cnt_w > kf)

            @pl.when(any_true(excess))
            def _():
                need = jnp.where(excess, kf - above, float(score_ref.shape[0]))
                r_i = lax.broadcasted_iota(jnp.int32, (Q_TILE, Q_TILE), 0)
                c_i = lax.broadcasted_iota(jnp.int32, (Q_TILE, Q_TILE), 1)
                earlier = (c_i < r_i).astype(BF16)

                def demote(j, before):
                    c = n_chunks - 1 - j
                    x = score_ref[chunk_rows(c), :]
                    tied = jnp.where(x == thr_w, 1.0, 0.0)
                    ranks = []
                    for blk in range(K_CHUNK // Q_TILE):
                        t_blk = tied[blk * Q_TILE:(blk + 1) * Q_TILE]
                        ranks.append(before + jnp.dot(earlier, t_blk.astype(BF16), preferred_element_type=F32))
                        before = before + jnp.sum(_fold(t_blk, jnp.sum), axis=0, keepdims=True)
                    drop = jnp.logical_and(x == thr_w, jnp.concatenate(ranks, axis=0) >= need)
                    score_ref[chunk_rows(c), :] = jnp.where(drop, -jnp.inf, x)
                    return before

                lax.fori_loop(0, n_chunks, demote, jnp.zeros((1, Q_TILE), F32))

            return thr_w

        thr = lax.cond(any_true(pending), walk_down, lambda: lo)
        return jnp.where(active, thr, floor)

    thr = lax.cond(t0 + Q_TILE > topk, search, lambda: floor)

    aq_pairs = pair_rows(aq_ref[...])
    near = K_CHUNK - 2 * Q_TILE
    m_ref[...] = jnp.full(m_ref.shape, NEG_BIG, F32)
    acc_ref[...] = jnp.zeros(acc_ref.shape, F32)
    ones_rows = jnp.ones((SUM_ROWS, K_CHUNK), BF16)

    def logits(c, with_bias, dst_ref):
        start = key_start(c)
        madd = jnp.where(score_ref[chunk_rows(c), :] >= thr, 0.0, NEG_BIG)
        for p in range(n_pairs):
            kp = ak_ref[pl.ds(start, K_CHUNK), p * LANES:(p + 1) * LANES]
            s = lax.dot_general(kp, aq_pairs[p], _NT, preferred_element_type=F32)
            for e in range(2):
                se = s[:, e * Q_TILE:(e + 1) * Q_TILE] + madd
                if with_bias and near:
                    se = jnp.concatenate([se[:near], se[near:] + tb_ref[2 * p + e]], axis=0)
                elif with_bias:
                    se = se + tb_ref[2 * p + e]
                dst_ref[p, :, e * Q_TILE:(e + 1) * Q_TILE] = se

    def update(c, src_ref):
        start = key_start(c)
        for p in range(n_pairs):
            vt = jnp.concatenate([avt_ref[p * LANES:(p + 1) * LANES, pl.ds(start, K_CHUNK)], ones_rows], axis=0)
            s = src_ref[p]
            m_old = m_ref[p]
            m_new = jnp.maximum(m_old, jnp.max(_fold(s, jnp.max), axis=0, keepdims=True))
            alpha = jnp.exp2(m_old - m_new)
            pe = jnp.exp2(s - m_new)
            acc_ref[p] = alpha * acc_ref[p] + jnp.dot(vt, pe.astype(BF16), preferred_element_type=F32)
            m_ref[p] = m_new

    def step(c, cur_ref, next_ref):
        logits(c + 1, False, next_ref)
        update(c, cur_ref)

    def pipelined(c, _):
        lax.cond(c % 2 == 0, lambda: step(c, la_ref, lb_ref), lambda: step(c, lb_ref, la_ref))
        return 0

    logits(0, True, la_ref)
    lax.fori_loop(0, n_chunks - 1, pipelined, 0)
    last = n_chunks - 1
    lax.cond(last % 2 == 0, lambda: update(last, la_ref), lambda: update(last, lb_ref))

    for p in range(n_pairs):
        acc = acc_ref[p]
        inv = 1.0 / acc[LANES:LANES + 1, :]
        o = jnp.concatenate([acc[:A_DH, :Q_TILE] * inv[:, :Q_TILE], acc[A_DH:LANES, Q_TILE:] * inv[:, Q_TILE:]],
                            axis=0)
        out_ref[:, p * LANES:(p + 1) * LANES] = o.T.astype(BF16)


def _dsa(iq, gw, aq, kn, ak, avt, tb, topk):
    bsz, s, _ = iq.shape
    sp = s + K_PAD
    n_pairs = A_HEADS // 2
    assert s % (Q_TILE * Q_PER_STEP) == 0
    qblk = lambda width: pl.BlockSpec((None, Q_TILE * Q_PER_STEP, width), lambda b, i: (b, i, 0))
    kblk = lambda width: pl.BlockSpec((None, sp, width), lambda b, i: (b, 0, 0))
    return pl.pallas_call(
        functools.partial(_dsa_kernel, topk=topk),
        grid=(bsz, s // (Q_TILE * Q_PER_STEP)),
        in_specs=[qblk(A_WIDTH), qblk(LANES), qblk(A_WIDTH), kblk(LANES), kblk(A_WIDTH),
                  pl.BlockSpec((None, A_WIDTH, sp), lambda b, i: (b, 0, 0)),
                  pl.BlockSpec(tb.shape, lambda b, i: (0, 0, 0))],
        out_specs=qblk(A_WIDTH),
        out_shape=jax.ShapeDtypeStruct((bsz, s, A_WIDTH), BF16),
        scratch_shapes=[pltpu.VMEM((s, Q_TILE), F32),
                        pltpu.VMEM((n_pairs, K_CHUNK, 2 * Q_TILE), F32),
                        pltpu.VMEM((n_pairs, K_CHUNK, 2 * Q_TILE), F32),
                        pltpu.VMEM((n_pairs, LANES + SUM_ROWS, 2 * Q_TILE), F32),
                        pltpu.VMEM((n_pairs, 1, 2 * Q_TILE), F32)],
        compiler_params=pltpu.CompilerParams(dimension_semantics=("parallel", "arbitrary"),
                                             vmem_limit_bytes=VMEM_LIMIT),
        name="dsa",
    )(iq, gw, aq, kn, ak, avt, tb)


def _rel_bias_table(rel_bias):
    c = jnp.arange(2 * Q_TILE, dtype=jnp.int32)[:, None]
    a = jnp.arange(Q_TILE, dtype=jnp.int32)[None, :]
    dist = Q_TILE + a - c
    max_exact = REL_BUCKETS // 2
    d = jnp.maximum(dist, 0)
    steps = (jnp.log(jnp.maximum(d, max_exact).astype(F32) / max_exact)
             / math.log(REL_MAX_DIST / max_exact) * (REL_BUCKETS - max_exact))
    large = jnp.minimum(max_exact + jnp.floor(steps).astype(jnp.int32), REL_BUCKETS - 1)
    bucket = jnp.where(d < max_exact, d, large)
    rb = rel_bias.astype(F32)
    onehot = (bucket[:, :, None] == jnp.arange(REL_BUCKETS, dtype=jnp.int32)).astype(F32)
    table = jnp.einsum("cak,kh->hca", onehot, rb - rb[REL_BUCKETS - 1], precision=lax.Precision.HIGHEST)
    return table * LOG2E


def _ffn_kernel(x_ref, hm_ref, ha_ref, xh_ref, hmh_ref, hah_ref, wo_ref, gf_ref,
                wu_ref, cw_ref, cb_ref, wd_ref, g_ref,
                out_ref, hext_ref, u_ref, a_ref, *, tiles_per_seq):
    tm = ROW_TILE

    def mixed(x, hm, ha):
        return (x + jnp.dot(hm, wo_ref[0:M_WIDTH, :], preferred_element_type=F32)
                + jnp.dot(ha, wo_ref[M_WIDTH:, :], preferred_element_type=F32))

    x1 = mixed(x_ref[...], hm_ref[...], ha_ref[...])
    halo = _rms(mixed(xh_ref[...], hmh_ref[...], hah_ref[...]), gf_ref[...]).astype(BF16)
    first = (pl.program_id(0) % tiles_per_seq) == 0
    hext_ref[0:FFN_HALO, :] = jnp.where(first, jnp.zeros_like(halo), halo)
    hext_ref[FFN_HALO:, :] = _rms(x1, gf_ref[...]).astype(BF16)
    hext = hext_ref[...]
    d_ff = wd_ref.shape[0]

    def conv(col):
        cs = slice(col, col + FF_CHUNK)
        u_ref[...] = jnp.dot(hext, wu_ref[:, cs], preferred_element_type=F32)
        y = cb_ref[:, cs] + u_ref[FFN_HALO:, :] * cw_ref[FFN_CONV - 1:FFN_CONV, cs]
        for j in range(FFN_CONV - 1):
            shift = FFN_CONV - 1 - j
            y = y + u_ref[FFN_HALO - shift:FFN_HALO - shift + tm, :] * cw_ref[j:j + 1, cs]
        return y

    for col in range(0, d_ff, FF_CHUNK):
        gate = conv(col)
        val = conv(d_ff + col)
        a_ref[:, col:col + FF_CHUNK] = (gate * _sigmoid(gate) * val).astype(BF16)

    x2 = x1 + jnp.dot(a_ref[...], wd_ref[...], preferred_element_type=F32)
    out_ref[...] = _rms(x2, g_ref[...])


def _ffn(x2d, hm, ha, wo, gf, wu, cw, cb, wd, g, tiles_per_seq):
    t, d = x2d.shape
    d_ff = wd.shape[0]
    assert d_ff % FF_CHUNK == 0 and wu.shape[1] == 2 * d_ff
    row = lambda i: (i, 0)
    halo = lambda i: (jnp.maximum(i * (ROW_TILE // FFN_HALO) - 1, 0), 0)
    fixed = lambda i: (0, 0)
    tile = lambda width: pl.BlockSpec((ROW_TILE, width), row)
    lead = lambda width: pl.BlockSpec((FFN_HALO, width), halo)
    whole = lambda a: pl.BlockSpec(a.shape, fixed)
    return pl.pallas_call(
        functools.partial(_ffn_kernel, tiles_per_seq=tiles_per_seq),
        grid=(t // ROW_TILE,),
        in_specs=[tile(d), tile(M_WIDTH), tile(A_WIDTH), lead(d), lead(M_WIDTH), lead(A_WIDTH),
                  whole(wo), whole(gf), whole(wu), whole(cw), whole(cb), whole(wd), whole(g)],
        out_specs=pl.BlockSpec((ROW_TILE, d), row),
        out_shape=jax.ShapeDtypeStruct((t, d), F32),
        scratch_shapes=[pltpu.VMEM((ROW_TILE + FFN_HALO, d), BF16),
                        pltpu.VMEM((ROW_TILE + FFN_HALO, FF_CHUNK), F32),
                        pltpu.VMEM((ROW_TILE, d_ff), BF16)],
        compiler_params=pltpu.CompilerParams(dimension_semantics=("parallel",), vmem_limit_bytes=VMEM_LIMIT),
        name="ffn",
    )(x2d, hm, ha, x2d, hm, ha, wo, gf, wu, cw, cb, wd, g)


def _forward(x, norm_mix, w_in, conv_w, conv_b, i_bias, f_bias, mlstm_norm, idx_k_norm, rel_bias,
             w_out, norm_ffn, w_up, ffn_conv_w, ffn_conv_b, w_down, norm_final):
    bsz, s, d = x.shape
    t = bsz * s
    assert t % ROW_TILE == 0 and s % ROW_TILE == 0 and s % M_CHUNK == 0 and s % K_CHUNK == 0
    topk = min(TOPK_MAX, s // 4)

    sizes = (M_HEADS * M_DQK, M_HEADS * M_DQK, M_WIDTH, M_WIDTH, M_HEADS, M_HEADS,
             A_WIDTH, A_WIDTH, A_WIDTH, IDX_HEADS * IDX_DIM, IDX_DIM, IDX_HEADS)
    offs = [0]
    for sz in sizes:
        offs.append(offs[-1] + sz)
    col = lambda k: w_in[:, offs[k]:offs[k + 1]]
    small_pad = jnp.zeros((d, LANES - 2 * M_HEADS - IDX_HEADS), w_in.dtype)
    w_cat = jnp.concatenate([col(0), col(1), col(2), col(3), col(6) * (A_DH ** -0.5 * LOG2E), col(7), col(8),
                             col(9) * (IDX_DIM ** -0.5), col(10), col(10), col(4), col(5), col(11), small_pad],
                            axis=1).astype(BF16)
    assert w_cat.shape[1] == IN_OFFS[-1]
    gk = jnp.concatenate([idx_k_norm, idx_k_norm]).reshape(1, LANES).astype(F32)

    qk, mv, mo, aq, iq, gw, ak, avt, kn = _in_proj(x, norm_mix.reshape(1, d).astype(F32), w_cat, gk)

    gate_bias = jnp.concatenate([i_bias, f_bias, jnp.zeros((LANES - 2 * M_HEADS,), F32)]).reshape(1, LANES)
    hm = _mlstm(qk, mv, mo, gw, conv_w.astype(F32), conv_b.reshape(1, -1).astype(F32),
                gate_bias, mlstm_norm.reshape(-1, 1).astype(F32))
    ha = _dsa(iq, gw, aq, kn, ak, avt, _rel_bias_table(rel_bias), topk)

    out = _ffn(x.reshape(t, d), hm.reshape(t, M_WIDTH), ha.reshape(t, A_WIDTH), w_out.astype(BF16),
               norm_ffn.reshape(1, d).astype(F32), w_up.astype(BF16), ffn_conv_w.astype(F32),
               ffn_conv_b.reshape(1, -1).astype(F32), w_down.astype(BF16),
               norm_final.reshape(1, d).astype(F32), s // ROW_TILE)
    return hm, ha, out.reshape(bsz, s, d)


def kernel(x, norm_mix, w_in, mlstm_conv_w, mlstm_conv_b, i_bias, f_bias, mlstm_norm, idx_k_norm, rel_bias,
           w_out, norm_ffn, w_up, ffn_conv_w, ffn_conv_b, w_down, norm_final):
    assert norm_mix.shape[0] == 1, "single-layer operation"
    return _forward(x, norm_mix[0], w_in[0], mlstm_conv_w[0], mlstm_conv_b[0], i_bias[0], f_bias[0],
                    mlstm_norm[0], idx_k_norm[0], rel_bias, w_out[0], norm_ffn[0], w_up[0], ffn_conv_w[0],
                    ffn_conv_b[0], w_down[0], norm_final)[2]
```
